```python
import jax, jax.numpy as jnp
from jax import lax
import numpy as np

D_MODEL = 4096
BATCH = 4
SEQ = 2048
DEPTH = 1
DEC_BATCH = 128
DEC_SEQ = 1
PAST_LEN = 16384
PAGE_SIZE = 128

CHUNK = 128
A_WIDTH = 2048
A_GROUPS = 4
RET_HEADS = 8
RET_HEAD_DIM = 256
RET_WIDTH = RET_HEADS * RET_HEAD_DIM
MEM_LEN = 256
MEM_HEADS = 4
MEM_HEAD_DIM = 256
MEM_WIDTH = MEM_HEADS * MEM_HEAD_DIM
N_BRANCH = 3
ROPE_BASE = 10000.0
EPS = 1e-6
SPLITS = (A_WIDTH, A_WIDTH, A_WIDTH, RET_WIDTH, RET_WIDTH, RET_WIDTH, RET_WIDTH,
          MEM_WIDTH, MEM_WIDTH, N_BRANCH * D_MODEL)
IN_WIDTH = sum(SPLITS)

kernel_name = "gated_gmlp_retention_memory_decoder_step"


def _chunk_len(s):
    return CHUNK if s % CHUNK == 0 else s


def rms_norm(x, g):
    xf = x.astype(jnp.float32)
    y = xf * lax.rsqrt(jnp.mean(xf * xf, axis=-1, keepdims=True) + EPS)
    return (y * g.astype(jnp.float32)).astype(x.dtype)


def layer_norm(x, g):
    xf = x.astype(jnp.float32)
    mu = jnp.mean(xf, axis=-1, keepdims=True)
    var = jnp.mean(jnp.square(xf - mu), axis=-1, keepdims=True)
    return ((xf - mu) * lax.rsqrt(var + EPS) * g.astype(jnp.float32)).astype(x.dtype)


def rotary(x, pos):
    half = x.shape[-1] // 2
    inv = ROPE_BASE ** (-jnp.arange(half, dtype=jnp.float32) / half)
    ang = pos.astype(jnp.float32)[:, None] * inv[None, :]
    cos = jnp.cos(ang)[None, :, None, :]
    sin = jnp.sin(ang)[None, :, None, :]
    xf = x.astype(jnp.float32)
    x1, x2 = xf[..., :half], xf[..., half:]
    return jnp.concatenate([x1 * cos - x2 * sin, x1 * sin + x2 * cos], axis=-1)


def chunk_spatial(v, w_s, b_s):
    B, S, _ = v.shape
    L = _chunk_len(S)
    n = S // L
    mask = jnp.tril(jnp.ones((L, L), dtype=bool))
    w = jnp.where(mask[None], w_s[:, :L, :L], 0.0).astype(v.dtype)
    vg = v.reshape(B, n, L, A_GROUPS, A_WIDTH // A_GROUPS)
    out = jnp.einsum('gij,bnjgc->bnigc', w, vg) + b_s[:, :L].T.astype(v.dtype)[None, None, :, :, None]
    return out.reshape(B, S, A_WIDTH)


def retention(q, k, v, state):
    B, S, H, _ = q.shape
    L = _chunk_len(S)
    n = S // L
    log_gamma = jnp.log1p(-jnp.exp2(-5.0 - jnp.arange(H, dtype=jnp.float32)))
    idx = jnp.arange(L, dtype=jnp.float32)
    diff = idx[:, None] - idx[None, :]
    decay_in = jnp.where(diff[None] >= 0, jnp.exp(log_gamma[:, None, None] * jnp.maximum(diff, 0.0)[None]), 0.0)
    decay_q = jnp.exp(log_gamma[:, None] * (idx[None, :] + 1.0))
    decay_k = jnp.exp(log_gamma[:, None] * (L - 1.0 - idx[None, :]))
    decay_blk = jnp.exp(log_gamma * L)

    def to_blocks(t):
        return t.astype(jnp.float32).reshape(B, n, L, H, t.shape[-1]).transpose(1, 0, 3, 2, 4)

    def step(R, inp):
        qi, ki, vi = inp
        s = jnp.einsum('bhid,bhjd->bhij', qi, ki) * decay_in[None]
        o = (jnp.einsum('bhij,bhjv->bhiv', s, vi)
             + jnp.einsum('bhid,bhdv->bhiv', qi, R) * decay_q[None, :, :, None])
        R = R * decay_blk[None, :, None, None] + jnp.einsum('bhjd,bhjv->bhdv', ki * decay_k[None, :, :, None], vi)
        return R, o

    R, o = lax.scan(step, state, (to_blocks(q), to_blocks(k), to_blocks(v)))
    o = o.transpose(1, 0, 3, 2, 4).reshape(B, S, H, v.shape[-1])
    return o, R


def memory_kv(mem, g_mem, w_mem_kv):
    B = mem.shape[0]
    kv = rms_norm(mem, g_mem) @ w_mem_kv
    k, v = jnp.split(kv, 2, axis=-1)
    return (k.reshape(B, MEM_LEN, MEM_HEADS, MEM_HEAD_DIM),
            v.reshape(B, MEM_LEN, MEM_HEADS, MEM_HEAD_DIM))


def mixer_layer(x, pos, ret_state, mem_k, mem_v, g_pre, w_in, g_anorm, w_s, b_s, g_ret,
                w_out_a, w_out_b, w_out_c, w_out):
    B, S, _ = x.shape
    h = rms_norm(x, g_pre)
    z = h @ w_in
    a_u, a_v, a_g, r_q, r_k, r_v, r_g, c_q, c_g, gates = jnp.split(z, np.cumsum(SPLITS)[:-1].tolist(), axis=-1)

    u = jax.nn.gelu(a_u)
    vn = layer_norm(jax.nn.gelu(a_v), g_anorm)
    y_a = (u * chunk_spatial(vn, w_s, b_s) * jax.nn.silu(a_g)) @ w_out_a

    q = rotary(r_q.reshape(B, S, RET_HEADS, RET_HEAD_DIM), pos)
    k = rotary(r_k.reshape(B, S, RET_HEADS, RET_HEAD_DIM), pos) * (RET_HEAD_DIM ** -0.5)
    v = r_v.reshape(B, S, RET_HEADS, RET_HEAD_DIM)
    o, R = retention(q, k, v, ret_state)
    mu = jnp.mean(o, axis=-1, keepdims=True)
    var = jnp.mean(jnp.square(o - mu), axis=-1, keepdims=True)
    o = (o - mu) * lax.rsqrt(var + EPS) * g_ret.astype(jnp.float32).reshape(RET_HEADS, RET_HEAD_DIM)
    y_b = (o.astype(x.dtype).reshape(B, S, RET_WIDTH) * jax.nn.silu(r_g)) @ w_out_b

    qm = c_q.reshape(B, S, MEM_HEADS, MEM_HEAD_DIM)
    sc = jnp.einsum('bshd,bmhd->bhsm', qm.astype(jnp.float32), mem_k.astype(jnp.float32)) * (MEM_HEAD_DIM ** -0.5)
    p = jax.nn.softmax(sc, axis=-1).astype(x.dtype)
    om = jnp.einsum('bhsm,bmhd->bshd', p, mem_v).reshape(B, S, MEM_WIDTH)
    y_c = (om * jax.nn.silu(c_g)) @ w_out_c

    gs = jax.nn.sigmoid(gates.astype(jnp.float32)).astype(x.dtype).reshape(B, S, N_BRANCH, D_MODEL)
    merged = gs[:, :, 0] * y_a + gs[:, :, 1] * y_b + gs[:, :, 2] * y_c
    return x + merged @ w_out, R.astype(ret_state.dtype), vn


def setup_inputs(seed: int = 0) -> dict:
    key = jax.random.key(seed)
    ks = jax.random.split(key, 24)
    f32 = jnp.float32
    nrm = lambda k, shape, s: jax.random.normal(k, shape, f32) * s
    return {
        "x_prompt": nrm(ks[0], (BATCH, SEQ, D_MODEL), 1.0),
        "x_sample": nrm(ks[1], (DEC_BATCH, DEC_SEQ, D_MODEL), 1.0),
        "state_ret": nrm(ks[2], (DEPTH, DEC_BATCH, RET_HEADS, RET_HEAD_DIM, RET_HEAD_DIM), RET_HEAD_DIM ** -0.5),
        "cache_mem_k": nrm(ks[3], (DEPTH, DEC_BATCH, MEM_LEN, MEM_HEADS, MEM_HEAD_DIM), 1.0),
        "cache_mem_v": nrm(ks[4], (DEPTH, DEC_BATCH, MEM_LEN, MEM_HEADS, MEM_HEAD_DIM), 1.0),
        "mem_prompt": nrm(ks[5], (BATCH, MEM_LEN, D_MODEL), 1.0),
        "g_pre": 1.0 + nrm(ks[6], (DEPTH, D_MODEL), 0.02),
        "w_in": nrm(ks[7], (DEPTH, D_MODEL, IN_WIDTH), D_MODEL ** -0.5),
        "g_anorm": 1.0 + nrm(ks[8], (DEPTH, A_WIDTH), 0.02),
        "w_s": nrm(ks[9], (DEPTH, A_GROUPS, CHUNK, CHUNK), 0.5 * CHUNK ** -0.5),
        "b_s": 1.0 + nrm(ks[10], (DEPTH, A_GROUPS, CHUNK), 0.02),
        "g_ret": 1.0 + nrm(ks[11], (DEPTH, RET_WIDTH), 0.02),
        "g_mem": 1.0 + nrm(ks[12], (DEPTH, D_MODEL), 0.02),
        "w_mem_kv": nrm(ks[13], (DEPTH, D_MODEL, 2 * MEM_WIDTH), D_MODEL ** -0.5),
        "w_out_a": nrm(ks[14], (DEPTH, A_WIDTH, D_MODEL), A_WIDTH ** -0.5),
        "w_out_b": nrm(ks[15], (DEPTH, RET_WIDTH, D_MODEL), RET_WIDTH ** -0.5),
        "w_out_c": nrm(ks[16], (DEPTH, MEM_WIDTH, D_MODEL), MEM_WIDTH ** -0.5),
        "w_out": nrm(ks[17], (DEPTH, D_MODEL, D_MODEL), 0.5 * D_MODEL ** -0.5),
        "g_final": 1.0 + nrm(ks[18], (D_MODEL,), 0.02),
    }


def reference(x_prompt, x_sample, state_ret, cache_mem_k, cache_mem_v, mem_prompt,
              g_pre, w_in, g_anorm, w_s, b_s, g_ret, g_mem, w_mem_kv,
              w_out_a, w_out_b, w_out_c, w_out, g_final):
    pos_p = jnp.arange(SEQ, dtype=jnp.int32)
    pos_s = PAST_LEN + jnp.arange(DEC_SEQ, dtype=jnp.int32)
    xp, xs = x_prompt, x_sample
    ret_p_all, mk_all, mv_all, ret_s_all, vs_all = [], [], [], [], []
    for l in range(DEPTH):
        shared = (g_pre[l], w_in[l], g_anorm[l], w_s[l], b_s[l], g_ret[l],
                  w_out_a[l], w_out_b[l], w_out_c[l], w_out[l])
        mk, mv = memory_kv(mem_prompt, g_mem[l], w_mem_kv[l])
        zero_state = jnp.zeros((BATCH, RET_HEADS, RET_HEAD_DIM, RET_HEAD_DIM), jnp.float32)
        xp, ret_p, _ = mixer_layer(xp, pos_p, zero_state, mk, mv, *shared)
        xs, ret_s, v_s = mixer_layer(xs, pos_s, state_ret[l], cache_mem_k[l], cache_mem_v[l], *shared)
        ret_p_all.append(ret_p)
        mk_all.append(mk)
        mv_all.append(mv)
        ret_s_all.append(ret_s)
        vs_all.append(v_s)
    y_prompt = rms_norm(xp, g_final)
    y_sample = rms_norm(xs, g_final)
    return (y_prompt, y_sample, jnp.stack(ret_p_all), jnp.stack(mk_all), jnp.stack(mv_all),
            jnp.stack(ret_s_all), jnp.stack(vs_all))
```

```python
import functools

import jax
import jax.numpy as jnp
from jax import lax
from jax.experimental import pallas as pl
from jax.experimental.pallas import tpu as pltpu

D_MODEL = 4096
BATCH = 4
SEQ = 2048
DEPTH = 1
DEC_BATCH = 128
DEC_SEQ = 1
PAST_LEN = 16384

CHUNK = 128
A_WIDTH = 2048
A_GROUPS = 4
RET_HEADS = 8
RET_HEAD_DIM = 256
RET_WIDTH = RET_HEADS * RET_HEAD_DIM
MEM_LEN = 256
MEM_HEADS = 4
MEM_HEAD_DIM = 256
MEM_WIDTH = MEM_HEADS * MEM_HEAD_DIM
N_BRANCH = 3
ROPE_BASE = 10000.0
EPS = 1e-6

OFF_AU = 0
OFF_AV = OFF_AU + A_WIDTH
OFF_AG = OFF_AV + A_WIDTH
OFF_RQ = OFF_AG + A_WIDTH
OFF_RK = OFF_RQ + RET_WIDTH
OFF_RV = OFF_RK + RET_WIDTH
OFF_RG = OFF_RV + RET_WIDTH
OFF_CQ = OFF_RG + RET_WIDTH
OFF_CG = OFF_CQ + MEM_WIDTH
OFF_GATES = OFF_CG + MEM_WIDTH
IN_WIDTH = OFF_GATES + N_BRANCH * D_MODEL

M_MAIN = BATCH * SEQ
M_TAIL = DEC_BATCH * DEC_SEQ
ROPE_HALF = RET_HEAD_DIM // 2

F32 = jnp.float32
BF16 = jnp.bfloat16

MIB = 1024 * 1024
VMEM_BUDGET_BYTES = 56 * MIB


def _params(semantics, vmem_bytes=VMEM_BUDGET_BYTES):
    return pltpu.CompilerParams(dimension_semantics=semantics, vmem_limit_bytes=vmem_bytes)


def _rmsnorm_kernel(x_ref, g_ref, o_ref):
    x = x_ref[...].astype(F32)
    y = x * lax.rsqrt(jnp.mean(x * x, axis=-1, keepdims=True) + EPS)
    o_ref[...] = (y * g_ref[...].astype(F32)).astype(o_ref.dtype)


def _rmsnorm(x, g, rows, out_dtype):
    m, d = x.shape
    return pl.pallas_call(
        _rmsnorm_kernel,
        grid=(m // rows,),
        in_specs=[pl.BlockSpec((rows, d), lambda i: (i, 0)),
                  pl.BlockSpec((1, d), lambda i: (0, 0))],
        out_specs=pl.BlockSpec((rows, d), lambda i: (i, 0)),
        out_shape=jax.ShapeDtypeStruct((m, d), out_dtype),
        compiler_params=_params(("arbitrary",)),
        name="rmsnorm",
    )(x, g.reshape(1, d))


def _rope_kernel(cm_ref, sm_ref, ct_ref, st_ref):
    def table(rows, pos):
        j = lax.broadcasted_iota(jnp.int32, (rows, ROPE_HALF), 1).astype(F32)
        inv = ROPE_BASE ** (-j / ROPE_HALF)
        return pos.astype(F32) * inv

    ang = table(SEQ, lax.broadcasted_iota(jnp.int32, (SEQ, ROPE_HALF), 0))
    cm_ref[...] = jnp.cos(ang)
    sm_ref[...] = jnp.sin(ang)
    r = lax.broadcasted_iota(jnp.int32, (M_TAIL, ROPE_HALF), 0)
    t = jnp.zeros_like(r) if DEC_SEQ == 1 else lax.rem(r, DEC_SEQ)
    ang_t = table(M_TAIL, PAST_LEN + t)
    ct_ref[...] = jnp.cos(ang_t)
    st_ref[...] = jnp.sin(ang_t)


def _rope_tables():
    return pl.pallas_call(
        _rope_kernel,
        out_shape=(jax.ShapeDtypeStruct((SEQ, ROPE_HALF), F32),
                   jax.ShapeDtypeStruct((SEQ, ROPE_HALF), F32),
                   jax.ShapeDtypeStruct((M_TAIL, ROPE_HALF), F32),
                   jax.ShapeDtypeStruct((M_TAIL, ROPE_HALF), F32)),
        name="rope_tables",
    )()


CAST_ROWS = 256


def _wres_kernel(*refs, lhs_of, n_ext, n_out, mt, has_tail, sub, epilogue):
    refs = list(refs)
    n_pairs, n_lhs = len(lhs_of), max(lhs_of) + 1

    def take(k):
        out = refs[:k]
        del refs[:k]
        return out

    lhs_main = take(n_lhs)
    lhs_tail = take(n_lhs) if has_tail else []
    w = take(n_pairs)
    ext_main = take(n_ext)
    ext_tail = take(n_ext) if has_tail else []
    out_main = take(n_out)
    out_tail = take(n_out) if has_tail else []
    wb = take(n_pairs)

    n = pl.program_id(0)
    m = pl.program_id(1)

    @pl.when(m == 0)
    def _cast():
        for w_ref, wb_ref in zip(w, wb):
            def body(i, carry, w_ref=w_ref, wb_ref=wb_ref):
                r = pl.multiple_of(i * CAST_ROWS, CAST_ROWS)
                wb_ref[pl.ds(r, CAST_ROWS), :] = w_ref[pl.ds(r, CAST_ROWS), :].astype(BF16)
                return carry
            lax.fori_loop(0, w_ref.shape[0] // CAST_ROWS, body, 0)

    def body(lhs, ext, outs):
        tn = wb[0].shape[1]
        for c in range(0, tn, sub):
            accs = [jnp.dot(lhs[i][...].astype(BF16), b[:, c:c + sub], preferred_element_type=F32)
                    for i, b in zip(lhs_of, wb)]
            res = epilogue(accs, ext, c, sub, n)
            for o, r in zip(outs, res):
                o[:, c:c + sub] = r.astype(o.dtype)

    if has_tail:
        @pl.when(m < mt)
        def _main():
            body(lhs_main, ext_main, out_main)

        @pl.when(m == mt)
        def _tail():
            body(lhs_tail, ext_tail, out_tail)
    else:
        body(lhs_main, ext_main, out_main)


def _wres_matmul(pairs, exts, outs, *, n_cols, tn, tm, sub, epilogue, name):
    m_main = pairs[0][0].shape[0]
    has_tail = pairs[0][1] is not None
    mt = m_main // tm
    n_tiles = n_cols // tn
    grid = (n_tiles, mt + (1 if has_tail else 0))

    def mrow(m):
        return jnp.minimum(m, mt - 1)

    lhs, lhs_of = [], []
    for lm, lt, _, _ in pairs:
        ids = [i for i, (a, _) in enumerate(lhs) if a is lm]
        if not ids:
            lhs.append((lm, lt))
        lhs_of.append(ids[0] if ids else len(lhs) - 1)

    in_specs, args = [], []
    for lm, _ in lhs:
        in_specs.append(pl.BlockSpec((tm, lm.shape[1]), lambda n, m: (mrow(m), 0)))
        args.append(lm)
    if has_tail:
        for _, lt in lhs:
            in_specs.append(pl.BlockSpec(lt.shape, lambda n, m: (0, 0)))
            args.append(lt)
    for _, _, w, off in pairs:
        assert off % tn == 0
        in_specs.append(pl.BlockSpec((w.shape[0], tn), lambda n, m, o=off // tn: (0, o + n)))
        args.append(w)

    def ext_spec(arr, kind, off, tail):
        rows = arr.shape[0] if tail else tm
        if kind == "tile":
            assert off % tn == 0
            if tail:
                return pl.BlockSpec((rows, tn), lambda n, m, o=off // tn: (0, o + n))
            return pl.BlockSpec((rows, tn), lambda n, m, o=off // tn: (mrow(m), o + n))
        assert kind == "rope"
        if tail:
            return pl.BlockSpec(arr.shape, lambda n, m: (0, 0))
        per = arr.shape[0] // tm
        return pl.BlockSpec((tm, arr.shape[1]), lambda n, m: (lax.rem(mrow(m), per), 0))

    for em, _, kind, off in exts:
        in_specs.append(ext_spec(em, kind, off, False))
        args.append(em)
    if has_tail:
        for _, et, kind, off in exts:
            in_specs.append(ext_spec(et, kind, off, True))
            args.append(et)

    out_specs, out_shape = [], []
    for dm, _ in outs:
        out_specs.append(pl.BlockSpec((tm, tn), lambda n, m: (mrow(m), n)))
        out_shape.append(jax.ShapeDtypeStruct((m_main, n_cols), dm))
    if has_tail:
        m_tail = pairs[0][1].shape[0]
        for _, dt in outs:
            out_specs.append(pl.BlockSpec((m_tail, tn), lambda n, m: (0, n)))
            out_shape.append(jax.ShapeDtypeStruct((m_tail, n_cols), dt))

    scratch = [pltpu.VMEM((w.shape[0], tn), BF16) for _, _, w, _ in pairs]
    kern = functools.partial(_wres_kernel, lhs_of=tuple(lhs_of), n_ext=len(exts), n_out=len(outs),
                             mt=mt, has_tail=has_tail, sub=sub, epilogue=epilogue)
    return pl.pallas_call(
        kern, grid=grid, in_specs=in_specs, out_specs=out_specs, out_shape=out_shape,
        scratch_shapes=scratch, compiler_params=_params(("arbitrary", "arbitrary")), name=name,
    )(*args)


def _epi_identity(accs, ext, c, sub, n):
    return [accs[0]]


def _epi_gelu(accs, ext, c, sub, n):
    return [jax.nn.gelu(accs[0])]


def _epi_silu(accs, ext, c, sub, n):
    return [jax.nn.silu(accs[0])]


def _epi_sigmoid(accs, ext, c, sub, n):
    return [jax.nn.sigmoid(accs[0])]


def _epi_gelu_silu(accs, ext, c, sub, n):
    return [jax.nn.gelu(accs[0]) * jax.nn.silu(accs[1])]


def _epi_rope(scale, accs, ext, c, sub, n):
    assert sub == RET_HEAD_DIM
    x = accs[0]
    x1, x2 = x[:, :ROPE_HALF], x[:, ROPE_HALF:]
    cos, sin = ext[0][...], ext[1][...]
    out = jnp.concatenate([x1 * cos - x2 * sin, x1 * sin + x2 * cos], axis=-1)
    return [out if scale == 1.0 else out * scale]


def _epi_merge(accs, ext, c, sub, n):
    g0, g1, g2 = (e[:, c:c + sub] for e in ext)
    return [g0 * accs[0] + g1 * accs[1] + g2 * accs[2]]


def _epi_residual(accs, ext, c, sub, n):
    return [ext[0][:, c:c + sub] + accs[0]]


def _layer_norm(x, g):
    mu = jnp.mean(x, axis=-1, keepdims=True)
    var = jnp.mean(jnp.square(x - mu), axis=-1, keepdims=True)
    return (x - mu) * lax.rsqrt(var + EPS) * g


def _spatial_main_kernel(ug_ref, gv_ref, ga_ref, ws_ref, bst_ref, o_ref):
    rows = gv_ref.shape[0]
    gw = A_WIDTH // A_GROUPS
    vn = _layer_norm(gv_ref[...], ga_ref[...]).astype(BF16)
    ri = lax.broadcasted_iota(jnp.int32, (CHUNK, CHUNK), 0)
    ci = lax.broadcasted_iota(jnp.int32, (CHUNK, CHUNK), 1)
    for g in range(A_GROUPS):
        w = jnp.where(ri >= ci, ws_ref[g], 0.0).astype(BF16)
        bias = bst_ref[:, g:g + 1]
        for c in range(rows // CHUNK):
            rs = slice(c * CHUNK, (c + 1) * CHUNK)
            cs = slice(g * gw, (g + 1) * gw)
            sp = jnp.dot(w, vn[rs, cs], preferred_element_type=F32) + bias
            o_ref[rs, cs] = (ug_ref[rs, cs] * sp).astype(o_ref.dtype)


def _spatial_main(ug, gv, g_anorm, w_s, b_s, rows=512):
    m = ug.shape[0]
    return pl.pallas_call(
        _spatial_main_kernel,
        grid=(m // rows,),
        in_specs=[pl.BlockSpec((rows, A_WIDTH), lambda i: (i, 0)),
                  pl.BlockSpec((rows, A_WIDTH), lambda i: (i, 0)),
                  pl.BlockSpec((1, A_WIDTH), lambda i: (0, 0)),
                  pl.BlockSpec((A_GROUPS, CHUNK, CHUNK), lambda i: (0, 0, 0)),
                  pl.BlockSpec((CHUNK, A_GROUPS), lambda i: (0, 0))],
        out_specs=pl.BlockSpec((rows, A_WIDTH), lambda i: (i, 0)),
        out_shape=jax.ShapeDtypeStruct((m, A_WIDTH), BF16),
        compiler_params=_params(("arbitrary",)),
        name="spatial_main",
    )(ug, gv, g_anorm.reshape(1, A_WIDTH), w_s, b_s.T)


def _spatial_tail_kernel(ug_ref, gv_ref, ga_ref, ws_ref, bs_ref, o_ref, vn_ref):
    gw = A_WIDTH // A_GROUPS
    vn = _layer_norm(gv_ref[...], ga_ref[...])
    vn_ref[...] = vn
    for g in range(A_GROUPS):
        cs = slice(g * gw, (g + 1) * gw)
        sp = vn[:, cs] * ws_ref[g, 0:1, 0:1] + bs_ref[g:g + 1, 0:1]
        o_ref[:, cs] = (ug_ref[:, cs] * sp).astype(o_ref.dtype)


def _spatial_tail(ug, gv, g_anorm, w_s, b_s):
    m = ug.shape[0]
    return pl.pallas_call(
        _spatial_tail_kernel,
        out_shape=(jax.ShapeDtypeStruct((m, A_WIDTH), BF16),
                   jax.ShapeDtypeStruct((m, A_WIDTH), F32)),
        compiler_params=_params(None),
        name="spatial_tail",
    )(ug, gv, g_anorm.reshape(1, A_WIDTH), w_s, b_s)


def _log_gamma(shape, head):
    return jnp.log1p(-jnp.exp2(jnp.full(shape, -5.0, F32) - head.astype(F32)))


def _group_norm(o, g):
    mu = jnp.mean(o, axis=-1, keepdims=True)
    var = jnp.mean(jnp.square(o - mu), axis=-1, keepdims=True)
    return (o - mu) * lax.rsqrt(var + EPS) * g


def _ret_main_kernel(q_ref, k_ref, v_ref, sg_ref, gr_ref, o_ref, r_ref, state):
    head = pl.program_id(1)
    L, dk = CHUNK, RET_HEAD_DIM
    ri = lax.broadcasted_iota(jnp.int32, (L, L), 0).astype(F32)
    ci = lax.broadcasted_iota(jnp.int32, (L, L), 1).astype(F32)
    diff = ri - ci
    decay_in = jnp.where(diff >= 0, jnp.exp(_log_gamma((L, L), head) * jnp.maximum(diff, 0.0)), 0.0)
    rw = lax.broadcasted_iota(jnp.int32, (L, dk), 0).astype(F32)
    lg_w = _log_gamma((L, dk), head)
    decay_q = jnp.exp(lg_w * (rw + 1.0))
    decay_k = jnp.exp(lg_w * (L - 1.0 - rw))
    decay_blk = jnp.exp(_log_gamma((dk, dk), head) * L)
    gr = gr_ref[...]
    state[...] = jnp.zeros_like(state)

    def body(i, carry):
        r0 = pl.multiple_of(i * L, L)
        qi = q_ref[pl.ds(r0, L), :]
        ki = k_ref[pl.ds(r0, L), :]
        vi = v_ref[pl.ds(r0, L), :]
        s = lax.dot_general(qi, ki, (((1,), (1,)), ((), ())), preferred_element_type=F32) * decay_in
        R = state[...]
        o = (jnp.dot(s.astype(BF16), vi, preferred_element_type=F32)
             + jnp.dot(qi, R.astype(BF16), preferred_element_type=F32) * decay_q)
        kd = (ki.astype(F32) * decay_k).astype(BF16)
        state[...] = R * decay_blk + lax.dot_general(
            kd, vi, (((0,), (0,)), ((), ())), preferred_element_type=F32)
        o_ref[pl.ds(r0, L), :] = (_group_norm(o, gr) * sg_ref[pl.ds(r0, L), :]).astype(o_ref.dtype)
        return carry

    lax.fori_loop(0, SEQ // L, body, 0)
    r_ref[...] = state[...]


def _ret_main(q, k, v, sg, g_ret):
    dk = RET_HEAD_DIM
    blk = lambda: pl.BlockSpec((SEQ, dk), lambda b, h: (b, h))
    return pl.pallas_call(
        _ret_main_kernel,
        grid=(BATCH, RET_HEADS),
        in_specs=[blk(), blk(), blk(), blk(), pl.BlockSpec((1, dk), lambda b, h: (0, h))],
        out_specs=(blk(), pl.BlockSpec((None, None, None, dk, dk), lambda b, h: (0, b, h, 0, 0))),
        out_shape=(jax.ShapeDtypeStruct((M_MAIN, RET_WIDTH), BF16),
                   jax.ShapeDtypeStruct((DEPTH, BATCH, RET_HEADS, dk, dk), F32)),
        scratch_shapes=[pltpu.VMEM((dk, dk), F32)],
        compiler_params=_params(("arbitrary", "arbitrary")),
        name="ret_main",
    )(q, k, v, sg, g_ret.reshape(1, RET_WIDTH))


RET_TAIL_ROWS = 16


def _ret_tail_kernel(q_ref, k_ref, v_ref, sg_ref, gr_ref, s_ref, o_ref, so_ref):
    head = pl.program_id(1)
    nb, dk = q_ref.shape
    gam_row = jnp.exp(_log_gamma((nb, dk), head))
    gam_st = jnp.exp(_log_gamma((dk, dk), head))
    q, k, v = q_ref[...], k_ref[...], v_ref[...]
    qb, vb = q.astype(BF16), v.astype(BF16)
    rows = lax.broadcasted_iota(jnp.int32, (nb, dk), 0)
    qr = jnp.zeros((nb, dk), F32)
    for r in range(nb):
        R = s_ref[r]
        qr = jnp.where(rows == r, jnp.dot(qb, R.astype(BF16), preferred_element_type=F32), qr)
        k_r = jnp.where(rows == r, k, 0.0).astype(BF16)
        so_ref[r] = R * gam_st + lax.dot_general(
            k_r, vb, (((0,), (0,)), ((), ())), preferred_element_type=F32)
    qk = jnp.sum(q * k, axis=-1, keepdims=True)
    o = qk * v + qr * gam_row
    o_ref[...] = _group_norm(o, gr_ref[...]) * sg_ref[...]


def _ret_tail(q, k, v, sg, g_ret, state):
    dk, nb = RET_HEAD_DIM, RET_TAIL_ROWS
    row = lambda: pl.BlockSpec((nb, dk), lambda i, h: (i, h))
    st = lambda: pl.BlockSpec((None, nb, None, dk, dk), lambda i, h: (0, i, h, 0, 0))
    return pl.pallas_call(
        _ret_tail_kernel,
        grid=(DEC_BATCH // nb, RET_HEADS),
        in_specs=[row(), row(), row(), row(), pl.BlockSpec((1, dk), lambda i, h: (0, h)), st()],
        out_specs=(row(), st()),
        out_shape=(jax.ShapeDtypeStruct((M_TAIL, RET_WIDTH), F32),
                   jax.ShapeDtypeStruct(state.shape, F32)),
        compiler_params=_params(("arbitrary", "arbitrary")),
        name="ret_tail",
    )(q, k, v, sg, g_ret.reshape(1, RET_WIDTH), state)


XATTN_ROWS = 512


def _xattn_main_kernel(q_ref, k_ref, v_ref, g_ref, o_ref):
    q = q_ref[...]
    k = k_ref[...].astype(BF16)
    v = v_ref[...].astype(BF16)
    sc = lax.dot_general(q, k, (((1,), (1,)), ((), ())), preferred_element_type=F32)
    sc = sc * (MEM_HEAD_DIM ** -0.5)
    e = jnp.exp(sc - jnp.max(sc, axis=-1, keepdims=True))
    p = e / jnp.sum(e, axis=-1, keepdims=True)
    om = jnp.dot(p.astype(BF16), v, preferred_element_type=F32)
    o_ref[...] = (om * g_ref[...]).astype(o_ref.dtype)


def _xattn_main(cq, mk, mv, scg):
    dh, rows = MEM_HEAD_DIM, XATTN_ROWS
    per = SEQ // rows
    qspec = lambda: pl.BlockSpec((rows, dh), lambda b, h, t: (b * per + t, h))
    kvspec = lambda: pl.BlockSpec((MEM_LEN, dh), lambda b, h, t: (b, h))
    return pl.pallas_call(
        _xattn_main_kernel,
        grid=(BATCH, MEM_HEADS, per),
        in_specs=[qspec(), kvspec(), kvspec(), qspec()],
        out_specs=qspec(),
        out_shape=jax.ShapeDtypeStruct((M_MAIN, MEM_WIDTH), BF16),
        compiler_params=_params(("arbitrary", "arbitrary", "arbitrary")),
        name="xattn_main",
    )(cq, mk, mv, scg)


XATTN_TAIL_ROWS = 8


def _xattn_tail_kernel(q_ref, g_ref, k_ref, v_ref, o_ref):
    dh = MEM_HEAD_DIM
    for r in range(q_ref.shape[0]):
        for h in range(MEM_HEADS):
            cs = slice(h * dh, (h + 1) * dh)
            q = q_ref[r:r + 1, cs]
            sc = jnp.sum(k_ref[r, :, cs] * q, axis=-1, keepdims=True) * (dh ** -0.5)
            e = jnp.exp(sc - jnp.max(sc, axis=0, keepdims=True))
            p = e / jnp.sum(e, axis=0, keepdims=True)
            om = jnp.sum(p * v_ref[r, :, cs], axis=0, keepdims=True)
            o_ref[r:r + 1, cs] = om * g_ref[r:r + 1, cs]


def _xattn_tail(cq, scg, ck, cv):
    nb = XATTN_TAIL_ROWS
    row = lambda: pl.BlockSpec((nb, MEM_WIDTH), lambda i: (i, 0))
    kv = lambda: pl.BlockSpec((nb, MEM_LEN, MEM_WIDTH), lambda i: (i, 0, 0))
    return pl.pallas_call(
        _xattn_tail_kernel,
        grid=(DEC_BATCH // nb,),
        in_specs=[row(), row(), kv(), kv()],
        out_specs=row(),
        out_shape=jax.ShapeDtypeStruct((M_TAIL, MEM_WIDTH), F32),
        compiler_params=_params(("arbitrary",)),
        name="xattn_tail",
    )(cq, scg, ck, cv)


def kernel(x_prompt, x_sample, state_ret, cache_mem_k, cache_mem_v, mem_prompt, g_pre, w_in, g_anorm,
           w_s, b_s, g_ret, g_mem, w_mem_kv, w_out_a, w_out_b, w_out_c, w_out, g_final):
    assert DEPTH == 1 and w_in.shape == (DEPTH, D_MODEL, IN_WIDTH)
    xp = x_prompt.reshape(M_MAIN, D_MODEL)
    xs = x_sample.reshape(M_TAIL, D_MODEL)
    win = w_in.reshape(D_MODEL, IN_WIDTH)

    h_m = _rmsnorm(xp, g_pre[0], 512, BF16)
    h_t = _rmsnorm(xs, g_pre[0], M_TAIL, BF16)
    cos_m, sin_m, cos_t, sin_t = _rope_tables()

    def inproj(offs, n_cols, epilogue, out_dtypes, name, tn=512, exts=()):
        return _wres_matmul([(h_m, h_t, win, o) for o in offs], list(exts), [out_dtypes],
                            n_cols=n_cols, tn=tn, tm=1024, sub=256, epilogue=epilogue, name=name)

    rope = [(cos_m, cos_t, "rope", 0), (sin_m, sin_t, "rope", 0)]
    ug_m, ug_t = inproj([OFF_AU, OFF_AG], A_WIDTH, _epi_gelu_silu, (F32, F32), "inproj_ug", tn=256)
    gv_m, gv_t = inproj([OFF_AV], A_WIDTH, _epi_gelu, (F32, F32), "inproj_gv")
    q_m, q_t = inproj([OFF_RQ], RET_WIDTH, functools.partial(_epi_rope, 1.0), (BF16, F32),
                      "inproj_q", exts=rope)
    k_m, k_t = inproj([OFF_RK], RET_WIDTH, functools.partial(_epi_rope, RET_HEAD_DIM ** -0.5),
                      (BF16, F32), "inproj_k", exts=rope)
    v_m, v_t = inproj([OFF_RV], RET_WIDTH, _epi_identity, (BF16, F32), "inproj_v")
    sg_m, sg_t = inproj([OFF_RG], RET_WIDTH, _epi_silu, (F32, F32), "inproj_rg")
    cq_m, cq_t = inproj([OFF_CQ], MEM_WIDTH, _epi_identity, (BF16, F32), "inproj_cq")
    scg_m, scg_t = inproj([OFF_CG], MEM_WIDTH, _epi_silu, (F32, F32), "inproj_cg")
    gs_m, gs_t = inproj([OFF_GATES], N_BRANCH * D_MODEL, _epi_sigmoid, (F32, F32), "inproj_gates")

    a_m = _spatial_main(ug_m, gv_m, g_anorm[0], w_s[0], b_s[0])
    a_t, vn_t = _spatial_tail(ug_t, gv_t, g_anorm[0], w_s[0], b_s[0])

    b_m, ret_p = _ret_main(q_m, k_m, v_m, sg_m, g_ret[0])
    b_t, ret_s = _ret_tail(q_t, k_t, v_t, sg_t, g_ret[0], state_ret)

    hm = _rmsnorm(mem_prompt.reshape(BATCH * MEM_LEN, D_MODEL), g_mem[0], 512, BF16)
    wkv = w_mem_kv.reshape(D_MODEL, 2 * MEM_WIDTH)
    (mk,) = _wres_matmul([(hm, None, wkv, 0)], [], [(F32, None)], n_cols=MEM_WIDTH, tn=512, tm=512,
                         sub=256, epilogue=_epi_identity, name="mem_k")
    (mv,) = _wres_matmul([(hm, None, wkv, MEM_WIDTH)], [], [(F32, None)], n_cols=MEM_WIDTH, tn=512,
                         tm=512, sub=256, epilogue=_epi_identity, name="mem_v")
    c_m = _xattn_main(cq_m, mk, mv, scg_m)
    c_t = _xattn_tail(cq_t, scg_t,
                      cache_mem_k.reshape(DEC_BATCH, MEM_LEN, MEM_WIDTH),
                      cache_mem_v.reshape(DEC_BATCH, MEM_LEN, MEM_WIDTH))

    merged_m, merged_t = _wres_matmul(
        [(a_m, a_t, w_out_a.reshape(A_WIDTH, D_MODEL), 0),
         (b_m, b_t, w_out_b.reshape(RET_WIDTH, D_MODEL), 0),
         (c_m, c_t, w_out_c.reshape(MEM_WIDTH, D_MODEL), 0)],
        [(gs_m, gs_t, "tile", b * D_MODEL) for b in range(N_BRANCH)],
        [(BF16, BF16)], n_cols=D_MODEL, tn=256, tm=1024, sub=256, epilogue=_epi_merge, name="merge")
    y_m, y_t = _wres_matmul(
        [(merged_m, merged_t, w_out.reshape(D_MODEL, D_MODEL), 0)],
        [(xp, xs, "tile", 0)], [(F32, F32)], n_cols=D_MODEL, tn=512, tm=1024, sub=256,
        epilogue=_epi_residual, name="outproj")
    y_prompt = _rmsnorm(y_m, g_final, 512, F32).reshape(BATCH, SEQ, D_MODEL)
    y_sample = _rmsnorm(y_t, g_final, M_TAIL, F32).reshape(DEC_BATCH, DEC_SEQ, D_MODEL)

    return (y_prompt, y_sample, ret_p,
            mk.reshape(DEPTH, BATCH, MEM_LEN, MEM_HEADS, MEM_HEAD_DIM),
            mv.reshape(DEPTH, BATCH, MEM_LEN, MEM_HEADS, MEM_HEAD_DIM),
            ret_s,
            vn_t.reshape(DEPTH, DEC_BATCH, DEC_SEQ, A_WIDTH))
```

```python
import functools

import jax
import jax.numpy as jnp
from jax import lax
from jax.experimental import pallas as pl
from jax.experimental.pallas import tpu as pltpu

D_MODEL = 4096
BATCH = 4
SEQ = 2048
DEPTH = 1
DEC_BATCH = 128
DEC_SEQ = 1
PAST_LEN = 16384

CHUNK = 128
A_WIDTH = 2048
A_GROUPS = 4
RET_HEADS = 8
RET_HEAD_DIM = 256
RET_WIDTH = RET_HEADS * RET_HEAD_DIM
MEM_LEN = 256
MEM_HEADS = 4
MEM_HEAD_DIM = 256
MEM_WIDTH = MEM_HEADS * MEM_HEAD_DIM
N_BRANCH = 3
ROPE_BASE = 10000.0
EPS = 1e-6

OFF_AU = 0
OFF_AV = OFF_AU + A_WIDTH
OFF_AG = OFF_AV + A_WIDTH
OFF_RQ = OFF_AG + A_WIDTH
OFF_RK = OFF_RQ + RET_WIDTH
OFF_RV = OFF_RK + RET_WIDTH
OFF_RG = OFF_RV + RET_WIDTH
OFF_CQ = OFF_RG + RET_WIDTH
OFF_CG = OFF_CQ + MEM_WIDTH
OFF_GATES = OFF_CG + MEM_WIDTH
IN_WIDTH = OFF_GATES + N_BRANCH * D_MODEL

M_MAIN = BATCH * SEQ
M_TAIL = DEC_BATCH * DEC_SEQ
ROPE_HALF = RET_HEAD_DIM // 2

F32 = jnp.float32
BF16 = jnp.bfloat16

MIB = 1024 * 1024
VMEM_BUDGET_BYTES = 56 * MIB


def _params(semantics, vmem_bytes=VMEM_BUDGET_BYTES):
    return pltpu.CompilerParams(dimension_semantics=semantics, vmem_limit_bytes=vmem_bytes)


def _rmsnorm_kernel(x_ref, g_ref, o_ref):
    x = x_ref[...].astype(F32)
    y = x * lax.rsqrt(jnp.mean(x * x, axis=-1, keepdims=True) + EPS)
    o_ref[...] = (y * g_ref[...].astype(F32)).astype(o_ref.dtype)


def _rmsnorm(x, g, rows, out_dtype):
    m, d = x.shape
    return pl.pallas_call(
        _rmsnorm_kernel,
        grid=(m // rows,),
        in_specs=[pl.BlockSpec((rows, d), lambda i: (i, 0)),
                  pl.BlockSpec((1, d), lambda i: (0, 0))],
        out_specs=pl.BlockSpec((rows, d), lambda i: (i, 0)),
        out_shape=jax.ShapeDtypeStruct((m, d), out_dtype),
        compiler_params=_params(("arbitrary",)),
        name="rmsnorm",
    )(x, g.reshape(1, d))


def _rope_kernel(cm_ref, sm_ref, ct_ref, st_ref):
    def table(rows, pos):
        j = lax.broadcasted_iota(jnp.int32, (rows, ROPE_HALF), 1).astype(F32)
        inv = ROPE_BASE ** (-j / ROPE_HALF)
        return pos.astype(F32) * inv

    ang = table(SEQ, lax.broadcasted_iota(jnp.int32, (SEQ, ROPE_HALF), 0))
    cm_ref[...] = jnp.cos(ang)
    sm_ref[...] = jnp.sin(ang)
    r = lax.broadcasted_iota(jnp.int32, (M_TAIL, ROPE_HALF), 0)
    t = jnp.zeros_like(r) if DEC_SEQ == 1 else lax.rem(r, DEC_SEQ)
    ang_t = table(M_TAIL, PAST_LEN + t)
    ct_ref[...] = jnp.cos(ang_t)
    st_ref[...] = jnp.sin(ang_t)


def _rope_tables():
    return pl.pallas_call(
        _rope_kernel,
        out_shape=(jax.ShapeDtypeStruct((SEQ, ROPE_HALF), F32),
                   jax.ShapeDtypeStruct((SEQ, ROPE_HALF), F32),
                   jax.ShapeDtypeStruct((M_TAIL, ROPE_HALF), F32),
                   jax.ShapeDtypeStruct((M_TAIL, ROPE_HALF), F32)),
        name="rope_tables",
    )()


CAST_ROWS = 256


def _wres_kernel(*refs, lhs_of, n_ext, n_out, mt, has_tail, sub, epilogue):
    refs = list(refs)
    n_pairs, n_lhs = len(lhs_of), max(lhs_of) + 1

    def take(k):
        out = refs[:k]
        del refs[:k]
        return out

    lhs_main = take(n_lhs)
    lhs_tail = take(n_lhs) if has_tail else []
    w = take(n_pairs)
    ext_main = take(n_ext)
    ext_tail = take(n_ext) if has_tail else []
    out_main = take(n_out)
    out_tail = take(n_out) if has_tail else []
    wb = take(n_pairs)

    n = pl.program_id(0)
    m = pl.program_id(1)

    @pl.when(m == 0)
    def _cast():
        for w_ref, wb_ref in zip(w, wb):
            def body(i, carry, w_ref=w_ref, wb_ref=wb_ref):
                r = pl.multiple_of(i * CAST_ROWS, CAST_ROWS)
                wb_ref[pl.ds(r, CAST_ROWS), :] = w_ref[pl.ds(r, CAST_ROWS), :].astype(BF16)
                return carry
            lax.fori_loop(0, w_ref.shape[0] // CAST_ROWS, body, 0)

    def body(lhs, ext, outs):
        tn = wb[0].shape[1]
        for c in range(0, tn, sub):
            accs = [jnp.dot(lhs[i][...].astype(BF16), b[:, c:c + sub], preferred_element_type=F32)
                    for i, b in zip(lhs_of, wb)]
            res = epilogue(accs, ext, c, sub, n)
            for o, r in zip(outs, res):
                o[:, c:c + sub] = r.astype(o.dtype)

    body(lhs_main, ext_main, out_main)
    if has_tail:
        @pl.when(m == mt - 1)
        def _tail():
            body(lhs_tail, ext_tail, out_tail)


def _wres_matmul(pairs, exts, outs, *, n_cols, tn, tm, sub, epilogue, name):
    m_main = pairs[0][0].shape[0]
    has_tail = pairs[0][1] is not None
    mt = m_main // tm
    n_tiles = n_cols // tn
    grid = (n_tiles, mt)

    def mrow(m):
        return m

    lhs, lhs_of = [], []
    for lm, lt, _, _ in pairs:
        ids = [i for i, (a, _) in enumerate(lhs) if a is lm]
        if not ids:
            lhs.append((lm, lt))
        lhs_of.append(ids[0] if ids else len(lhs) - 1)

    in_specs, args = [], []
    for lm, _ in lhs:
        in_specs.append(pl.BlockSpec((tm, lm.shape[1]), lambda n, m: (mrow(m), 0)))
        args.append(lm)
    if has_tail:
        for _, lt in lhs:
            in_specs.append(pl.BlockSpec(lt.shape, lambda n, m: (0, 0)))
            args.append(lt)
    for _, _, w, off in pairs:
        assert off % tn == 0
        in_specs.append(pl.BlockSpec((w.shape[0], tn), lambda n, m, o=off // tn: (0, o + n)))
        args.append(w)

    def ext_spec(arr, kind, off, tail):
        rows = arr.shape[0] if tail else tm
        if kind == "tile":
            assert off % tn == 0
            if tail:
                return pl.BlockSpec((rows, tn), lambda n, m, o=off // tn: (0, o + n))
            return pl.BlockSpec((rows, tn), lambda n, m, o=off // tn: (mrow(m), o + n))
        assert kind == "rope"
        if tail:
            return pl.BlockSpec(arr.shape, lambda n, m: (0, 0))
        per = arr.shape[0] // tm
        return pl.BlockSpec((tm, arr.shape[1]), lambda n, m: (lax.rem(mrow(m), per), 0))

    for em, _, kind, off in exts:
        in_specs.append(ext_spec(em, kind, off, False))
        args.append(em)
    if has_tail:
        for _, et, kind, off in exts:
            in_specs.append(ext_spec(et, kind, off, True))
            args.append(et)

    out_specs, out_shape = [], []
    for dm, _ in outs:
        out_specs.append(pl.BlockSpec((tm, tn), lambda n, m: (mrow(m), n)))
        out_shape.append(jax.ShapeDtypeStruct((m_main, n_cols), dm))
    if has_tail:
        m_tail = pairs[0][1].shape[0]
        for _, dt in outs:
            out_specs.append(pl.BlockSpec((m_tail, tn), lambda n, m: (0, n)))
            out_shape.append(jax.ShapeDtypeStruct((m_tail, n_cols), dt))

    scratch = [pltpu.VMEM((w.shape[0], tn), BF16) for _, _, w, _ in pairs]
    kern = functools.partial(_wres_kernel, lhs_of=tuple(lhs_of), n_ext=len(exts), n_out=len(outs),
                             mt=mt, has_tail=has_tail, sub=sub, epilogue=epilogue)
    return pl.pallas_call(
        kern, grid=grid, in_specs=in_specs, out_specs=out_specs, out_shape=out_shape,
        scratch_shapes=scratch, compiler_params=_params(("arbitrary", "arbitrary")), name=name,
    )(*args)


def _epi_identity(accs, ext, c, sub, n):
    return [accs[0]]


def _epi_gelu(accs, ext, c, sub, n):
    return [jax.nn.gelu(accs[0])]


def _epi_silu(accs, ext, c, sub, n):
    return [jax.nn.silu(accs[0])]


def _epi_sigmoid(accs, ext, c, sub, n):
    return [jax.nn.sigmoid(accs[0])]


def _epi_gelu_silu(accs, ext, c, sub, n):
    return [jax.nn.gelu(accs[0]) * jax.nn.silu(accs[1])]


def _epi_rope(scale, accs, ext, c, sub, n):
    assert sub == RET_HEAD_DIM
    x = accs[0]
    x1, x2 = x[:, :ROPE_HALF], x[:, ROPE_HALF:]
    cos, sin = ext[0][...], ext[1][...]
    out = jnp.concatenate([x1 * cos - x2 * sin, x1 * sin + x2 * cos], axis=-1)
    return [out if scale == 1.0 else out * scale]


def _epi_merge(accs, ext, c, sub, n):
    g0, g1, g2 = (e[:, c:c + sub] for e in ext)
    return [g0 * accs[0] + g1 * accs[1] + g2 * accs[2]]


def _epi_residual(accs, ext, c, sub, n):
    return [ext[0][:, c:c + sub] + accs[0]]


def _layer_norm(x, g):
    mu = jnp.mean(x, axis=-1, keepdims=True)
    var = jnp.mean(jnp.square(x - mu), axis=-1, keepdims=True)
    return (x - mu) * lax.rsqrt(var + EPS) * g


def _spatial_main_kernel(ug_ref, gv_ref, ga_ref, ws_ref, bst_ref, o_ref):
    rows = gv_ref.shape[0]
    gw = A_WIDTH // A_GROUPS
    vn = _layer_norm(gv_ref[...], ga_ref[...]).astype(BF16)
    ri = lax.broadcasted_iota(jnp.int32, (CHUNK, CHUNK), 0)
    ci = lax.broadcasted_iota(jnp.int32, (CHUNK, CHUNK), 1)
    for g in range(A_GROUPS):
        w = jnp.where(ri >= ci, ws_ref[g], 0.0).astype(BF16)
        bias = bst_ref[:, g:g + 1]
        for c in range(rows // CHUNK):
            rs = slice(c * CHUNK, (c + 1) * CHUNK)
            cs = slice(g * gw, (g + 1) * gw)
            sp = jnp.dot(w, vn[rs, cs], preferred_element_type=F32) + bias
            o_ref[rs, cs] = (ug_ref[rs, cs] * sp).astype(o_ref.dtype)


def _spatial_main(ug, gv, g_anorm, w_s, b_s, rows=512):
    m = ug.shape[0]
    return pl.pallas_call(
        _spatial_main_kernel,
        grid=(m // rows,),
        in_specs=[pl.BlockSpec((rows, A_WIDTH), lambda i: (i, 0)),
                  pl.BlockSpec((rows, A_WIDTH), lambda i: (i, 0)),
                  pl.BlockSpec((1, A_WIDTH), lambda i: (0, 0)),
                  pl.BlockSpec((A_GROUPS, CHUNK, CHUNK), lambda i: (0, 0, 0)),
                  pl.BlockSpec((CHUNK, A_GROUPS), lambda i: (0, 0))],
        out_specs=pl.BlockSpec((rows, A_WIDTH), lambda i: (i, 0)),
        out_shape=jax.ShapeDtypeStruct((m, A_WIDTH), BF16),
        compiler_params=_params(("arbitrary",)),
        name="spatial_main",
    )(ug, gv, g_anorm.reshape(1, A_WIDTH), w_s, b_s.T)


def _spatial_tail_kernel(ug_ref, gv_ref, ga_ref, ws_ref, bs_ref, o_ref, vn_ref):
    gw = A_WIDTH // A_GROUPS
    vn = _layer_norm(gv_ref[...], ga_ref[...])
    vn_ref[...] = vn
    for g in range(A_GROUPS):
        cs = slice(g * gw, (g + 1) * gw)
        sp = vn[:, cs] * ws_ref[g, 0:1, 0:1] + bs_ref[g:g + 1, 0:1]
        o_ref[:, cs] = (ug_ref[:, cs] * sp).astype(o_ref.dtype)


def _spatial_tail(ug, gv, g_anorm, w_s, b_s):
    m = ug.shape[0]
    return pl.pallas_call(
        _spatial_tail_kernel,
        out_shape=(jax.ShapeDtypeStruct((m, A_WIDTH), BF16),
                   jax.ShapeDtypeStruct((m, A_WIDTH), F32)),
        compiler_params=_params(None),
        name="spatial_tail",
    )(ug, gv, g_anorm.reshape(1, A_WIDTH), w_s, b_s)


def _log_gamma(shape, head):
    return jnp.log1p(-jnp.exp2(jnp.full(shape, -5.0, F32) - head.astype(F32)))


def _group_norm(o, g):
    mu = jnp.mean(o, axis=-1, keepdims=True)
    var = jnp.mean(jnp.square(o - mu), axis=-1, keepdims=True)
    return (o - mu) * lax.rsqrt(var + EPS) * g


def _ret_main_kernel(q_ref, k_ref, v_ref, sg_ref, gr_ref, o_ref, r_ref, state):
    head = pl.program_id(1)
    L, dk = CHUNK, RET_HEAD_DIM
    ri = lax.broadcasted_iota(jnp.int32, (L, L), 0).astype(F32)
    ci = lax.broadcasted_iota(jnp.int32, (L, L), 1).astype(F32)
    diff = ri - ci
    decay_in = jnp.where(diff >= 0, jnp.exp(_log_gamma((L, L), head) * jnp.maximum(diff, 0.0)), 0.0)
    rw = lax.broadcasted_iota(jnp.int32, (L, dk), 0).astype(F32)
    lg_w = _log_gamma((L, dk), head)
    decay_q = jnp.exp(lg_w * (rw + 1.0))
    decay_k = jnp.exp(lg_w * (L - 1.0 - rw))
    decay_blk = jnp.exp(_log_gamma((dk, dk), head) * L)
    gr = gr_ref[...]
    state[...] = jnp.zeros_like(state)

    def body(i, carry):
        r0 = pl.multiple_of(i * L, L)
        qi = q_ref[pl.ds(r0, L), :]
        ki = k_ref[pl.ds(r0, L), :]
        vi = v_ref[pl.ds(r0, L), :]
        s = lax.dot_general(qi, ki, (((1,), (1,)), ((), ())), preferred_element_type=F32) * decay_in
        R = state[...]
        o = (jnp.dot(s.astype(BF16), vi, preferred_element_type=F32)
             + jnp.dot(qi, R.astype(BF16), preferred_element_type=F32) * decay_q)
        kd = (ki.astype(F32) * decay_k).astype(BF16)
        state[...] = R * decay_blk + lax.dot_general(
            kd, vi, (((0,), (0,)), ((), ())), preferred_element_type=F32)
        o_ref[pl.ds(r0, L), :] = (_group_norm(o, gr) * sg_ref[pl.ds(r0, L), :]).astype(o_ref.dtype)
        return carry

    lax.fori_loop(0, SEQ // L, body, 0, unroll=8)
    r_ref[...] = state[...]


def _ret_main(q, k, v, sg, g_ret):
    dk = RET_HEAD_DIM
    blk = lambda: pl.BlockSpec((SEQ, dk), lambda b, h: (b, h))
    return pl.pallas_call(
        _ret_main_kernel,
        grid=(BATCH, RET_HEADS),
        in_specs=[blk(), blk(), blk(), blk(), pl.BlockSpec((1, dk), lambda b, h: (0, h))],
        out_specs=(blk(), pl.BlockSpec((None, None, None, dk, dk), lambda b, h: (0, b, h, 0, 0))),
        out_shape=(jax.ShapeDtypeStruct((M_MAIN, RET_WIDTH), BF16),
                   jax.ShapeDtypeStruct((DEPTH, BATCH, RET_HEADS, dk, dk), F32)),
        scratch_shapes=[pltpu.VMEM((dk, dk), F32)],
        compiler_params=_params(("arbitrary", "arbitrary")),
        name="ret_main",
    )(q, k, v, sg, g_ret.reshape(1, RET_WIDTH))


RET_TAIL_ROWS = 16


def _ret_tail_kernel(q_ref, k_ref, v_ref, sg_ref, gr_ref, s_ref, o_ref, so_ref):
    head = pl.program_id(1)
    nb, dk = q_ref.shape
    gam_row = jnp.exp(_log_gamma((nb, dk), head))
    gam_st = jnp.exp(_log_gamma((dk, dk), head))
    q, k, v = q_ref[...], k_ref[...], v_ref[...]
    qb, vb = q.astype(BF16), v.astype(BF16)
    rows = lax.broadcasted_iota(jnp.int32, (nb, dk), 0)
    qr = jnp.zeros((nb, dk), F32)
    for r in range(nb):
        R = s_ref[r]
        qr = jnp.where(rows == r, jnp.dot(qb, R.astype(BF16), preferred_element_type=F32), qr)
        k_r = jnp.where(rows == r, k, 0.0).astype(BF16)
        so_ref[r] = R * gam_st + lax.dot_general(
            k_r, vb, (((0,), (0,)), ((), ())), preferred_element_type=F32)
    qk = jnp.sum(q * k, axis=-1, keepdims=True)
    o = qk * v + qr * gam_row
    o_ref[...] = _group_norm(o, gr_ref[...]) * sg_ref[...]


def _ret_tail(q, k, v, sg, g_ret, state):
    dk, nb = RET_HEAD_DIM, RET_TAIL_ROWS
    row = lambda: pl.BlockSpec((nb, dk), lambda i, h: (i, h))
    st = lambda: pl.BlockSpec((None, nb, None, dk, dk), lambda i, h: (0, i, h, 0, 0))
    return pl.pallas_call(
        _ret_tail_kernel,
        grid=(DEC_BATCH // nb, RET_HEADS),
        in_specs=[row(), row(), row(), row(), pl.BlockSpec((1, dk), lambda i, h: (0, h)), st()],
        out_specs=(row(), st()),
        out_shape=(jax.ShapeDtypeStruct((M_TAIL, RET_WIDTH), F32),
                   jax.ShapeDtypeStruct(state.shape, F32)),
        compiler_params=_params(("arbitrary", "arbitrary")),
        name="ret_tail",
    )(q, k, v, sg, g_ret.reshape(1, RET_WIDTH), state)


XATTN_ROWS = 512


def _xattn_main_kernel(q_ref, k_ref, v_ref, g_ref, o_ref):
    k = k_ref[...].astype(BF16)
    v = v_ref[...].astype(BF16)
    for t in range(q_ref.shape[0] // XATTN_ROWS):
        rs = slice(t * XATTN_ROWS, (t + 1) * XATTN_ROWS)
        sc = lax.dot_general(q_ref[rs, :], k, (((1,), (1,)), ((), ())), preferred_element_type=F32)
        sc = sc * (MEM_HEAD_DIM ** -0.5)
        e = jnp.exp(sc - jnp.max(sc, axis=-1, keepdims=True))
        p = e / jnp.sum(e, axis=-1, keepdims=True)
        om = jnp.dot(p.astype(BF16), v, preferred_element_type=F32)
        o_ref[rs, :] = (om * g_ref[rs, :]).astype(o_ref.dtype)


def _xattn_main(cq, mk, mv, scg):
    dh = MEM_HEAD_DIM
    qspec = lambda: pl.BlockSpec((SEQ, dh), lambda b, h: (b, h))
    kvspec = lambda: pl.BlockSpec((MEM_LEN, dh), lambda b, h: (b, h))
    return pl.pallas_call(
        _xattn_main_kernel,
        grid=(BATCH, MEM_HEADS),
        in_specs=[qspec(), kvspec(), kvspec(), qspec()],
        out_specs=qspec(),
        out_shape=jax.ShapeDtypeStruct((M_MAIN, MEM_WIDTH), BF16),
        compiler_params=_params(("arbitrary", "arbitrary")),
        name="xattn_main",
    )(cq, mk, mv, scg)


XATTN_TAIL_ROWS = 4


def _xattn_tail_kernel(q_ref, g_ref, k_ref, v_ref, o_ref):
    for r in range(q_ref.shape[0]):
        q = q_ref[r]
        sc = jnp.sum(k_ref[r] * q[None], axis=-1, keepdims=True) * (MEM_HEAD_DIM ** -0.5)
        e = jnp.exp(sc - jnp.max(sc, axis=0, keepdims=True))
        p = e / jnp.sum(e, axis=0, keepdims=True)
        o_ref[r] = jnp.sum(p * v_ref[r], axis=0) * g_ref[r]


def _xattn_tail(cq, scg, ck, cv):
    nb = XATTN_TAIL_ROWS
    row = lambda: pl.BlockSpec((nb, MEM_HEADS, MEM_HEAD_DIM), lambda i: (i, 0, 0))
    kv = lambda: pl.BlockSpec((None, nb, MEM_LEN, MEM_HEADS, MEM_HEAD_DIM), lambda i: (0, i, 0, 0, 0))
    shape3 = (M_TAIL, MEM_HEADS, MEM_HEAD_DIM)
    out = pl.pallas_call(
        _xattn_tail_kernel,
        grid=(DEC_BATCH // nb,),
        in_specs=[row(), row(), kv(), kv()],
        out_specs=row(),
        out_shape=jax.ShapeDtypeStruct(shape3, F32),
        compiler_params=_params(("arbitrary",)),
        name="xattn_tail",
    )(cq.reshape(shape3), scg.reshape(shape3), ck, cv)
    return out.reshape(M_TAIL, MEM_WIDTH)


def kernel(x_prompt, x_sample, state_ret, cache_mem_k, cache_mem_v, mem_prompt, g_pre, w_in, g_anorm,
           w_s, b_s, g_ret, g_mem, w_mem_kv, w_out_a, w_out_b, w_out_c, w_out, g_final):
    assert DEPTH == 1 and w_in.shape == (DEPTH, D_MODEL, IN_WIDTH)
    xp = x_prompt.reshape(M_MAIN, D_MODEL)
    xs = x_sample.reshape(M_TAIL, D_MODEL)
    win = w_in.reshape(D_MODEL, IN_WIDTH)

    h_m = _rmsnorm(xp, g_pre[0], 512, BF16)
    h_t = _rmsnorm(xs, g_pre[0], M_TAIL, BF16)
    cos_m, sin_m, cos_t, sin_t = _rope_tables()

    def inproj(offs, n_cols, epilogue, out_dtypes, name, tn=512, exts=()):
        return _wres_matmul([(h_m, h_t, win, o) for o in offs], list(exts), [out_dtypes],
                            n_cols=n_cols, tn=tn, tm=1024, sub=256, epilogue=epilogue, name=name)

    rope = [(cos_m, cos_t, "rope", 0), (sin_m, sin_t, "rope", 0)]
    ug_m, ug_t = inproj([OFF_AU, OFF_AG], A_WIDTH, _epi_gelu_silu, (F32, F32), "inproj_ug", tn=256)
    gv_m, gv_t = inproj([OFF_AV], A_WIDTH, _epi_gelu, (F32, F32), "inproj_gv")
    q_m, q_t = inproj([OFF_RQ], RET_WIDTH, functools.partial(_epi_rope, 1.0), (BF16, F32),
                      "inproj_q", exts=rope)
    k_m, k_t = inproj([OFF_RK], RET_WIDTH, functools.partial(_epi_rope, RET_HEAD_DIM ** -0.5),
                      (BF16, F32), "inproj_k", exts=rope)
    v_m, v_t = inproj([OFF_RV], RET_WIDTH, _epi_identity, (BF16, F32), "inproj_v")
    sg_m, sg_t = inproj([OFF_RG], RET_WIDTH, _epi_silu, (F32, F32), "inproj_rg")
    cq_m, cq_t = inproj([OFF_CQ], MEM_WIDTH, _epi_identity, (BF16, F32), "inproj_cq")
    scg_m, scg_t = inproj([OFF_CG], MEM_WIDTH, _epi_silu, (F32, F32), "inproj_cg")
    gs_m, gs_t = inproj([OFF_GATES], N_BRANCH * D_MODEL, _epi_sigmoid, (BF16, BF16), "inproj_gates")

    a_m = _spatial_main(ug_m, gv_m, g_anorm[0], w_s[0], b_s[0])
    a_t, vn_t = _spatial_tail(ug_t, gv_t, g_anorm[0], w_s[0], b_s[0])

    b_m, ret_p = _ret_main(q_m, k_m, v_m, sg_m, g_ret[0])
    b_t, ret_s = _ret_tail(q_t, k_t, v_t, sg_t, g_ret[0], state_ret)

    hm = _rmsnorm(mem_prompt.reshape(BATCH * MEM_LEN, D_MODEL), g_mem[0], 512, BF16)
    wkv = w_mem_kv.reshape(D_MODEL, 2 * MEM_WIDTH)
    (mk,) = _wres_matmul([(hm, None, wkv, 0)], [], [(F32, None)], n_cols=MEM_WIDTH, tn=512, tm=512,
                         sub=256, epilogue=_epi_identity, name="mem_k")
    (mv,) = _wres_matmul([(hm, None, wkv, MEM_WIDTH)], [], [(F32, None)], n_cols=MEM_WIDTH, tn=512,
                         tm=512, sub=256, epilogue=_epi_identity, name="mem_v")
    c_m = _xattn_main(cq_m, mk, mv, scg_m)
    c_t = _xattn_tail(cq_t, scg_t, cache_mem_k, cache_mem_v)

    merged_m, merged_t = _wres_matmul(
        [(a_m, a_t, w_out_a.reshape(A_WIDTH, D_MODEL), 0),
         (b_m, b_t, w_out_b.reshape(RET_WIDTH, D_MODEL), 0),
         (c_m, c_t, w_out_c.reshape(MEM_WIDTH, D_MODEL), 0)],
        [(gs_m, gs_t, "tile", b * D_MODEL) for b in range(N_BRANCH)],
        [(BF16, BF16)], n_cols=D_MODEL, tn=512, tm=512, sub=256, epilogue=_epi_merge, name="merge")
    y_m, y_t = _wres_matmul(
        [(merged_m, merged_t, w_out.reshape(D_MODEL, D_MODEL), 0)],
        [(xp, xs, "tile", 0)], [(F32, F32)], n_cols=D_MODEL, tn=512, tm=1024, sub=256,
        epilogue=_epi_residual, name="outproj")
    y_prompt = _rmsnorm(y_m, g_final, 512, F32).reshape(BATCH, SEQ, D_MODEL)
    y_sample = _rmsnorm(y_t, g_final, M_TAIL, F32).reshape(DEC_BATCH, DEC_SEQ, D_MODEL)

    return (y_prompt, y_sample, ret_p,
            mk.reshape(DEPTH, BATCH, MEM_LEN, MEM_HEADS, MEM_HEAD_DIM),
            mv.reshape(DEPTH, BATCH, MEM_LEN, MEM_HEADS, MEM_HEAD_DIM),
            ret_s,
            vn_t.reshape(DEPTH, DEC_BATCH, DEC_SEQ, A_WIDTH))
```

```python
import functools

import jax
import jax.numpy as jnp
from jax import lax
from jax.experimental import pallas as pl
from jax.experimental.pallas import tpu as pltpu

D_MODEL = 4096
BATCH = 4
SEQ = 2048
DEPTH = 1
DEC_BATCH = 128
DEC_SEQ = 1
PAST_LEN = 16384

CHUNK = 128
A_WIDTH = 2048
A_GROUPS = 4
RET_HEADS = 8
RET_HEAD_DIM = 256
RET_WIDTH = RET_HEADS * RET_HEAD_DIM
MEM_LEN = 256
MEM_HEADS = 4
MEM_HEAD_DIM = 256
MEM_WIDTH = MEM_HEADS * MEM_HEAD_DIM
N_BRANCH = 3
ROPE_BASE = 10000.0
EPS = 1e-6

OFF_AU = 0
OFF_AV = OFF_AU + A_WIDTH
OFF_AG = OFF_AV + A_WIDTH
OFF_RQ = OFF_AG + A_WIDTH
OFF_RK = OFF_RQ + RET_WIDTH
OFF_RV = OFF_RK + RET_WIDTH
OFF_RG = OFF_RV + RET_WIDTH
OFF_CQ = OFF_RG + RET_WIDTH
OFF_CG = OFF_CQ + MEM_WIDTH
OFF_GATES = OFF_CG + MEM_WIDTH
IN_WIDTH = OFF_GATES + N_BRANCH * D_MODEL

M_MAIN = BATCH * SEQ
M_TAIL = DEC_BATCH * DEC_SEQ
ROPE_HALF = RET_HEAD_DIM // 2

F32 = jnp.float32
BF16 = jnp.bfloat16

MIB = 1024 * 1024
VMEM_BUDGET_BYTES = 56 * MIB


def _params(semantics, vmem_bytes=VMEM_BUDGET_BYTES):
    return pltpu.CompilerParams(dimension_semantics=semantics, vmem_limit_bytes=vmem_bytes)


def _rmsnorm_kernel(x_ref, g_ref, o_ref):
    x = x_ref[...].astype(F32)
    y = x * lax.rsqrt(jnp.mean(x * x, axis=-1, keepdims=True) + EPS)
    o_ref[...] = (y * g_ref[...].astype(F32)).astype(o_ref.dtype)


def _rmsnorm(x, g, rows, out_dtype):
    m, d = x.shape
    return pl.pallas_call(
        _rmsnorm_kernel,
        grid=(m // rows,),
        in_specs=[pl.BlockSpec((rows, d), lambda i: (i, 0)),
                  pl.BlockSpec((1, d), lambda i: (0, 0))],
        out_specs=pl.BlockSpec((rows, d), lambda i: (i, 0)),
        out_shape=jax.ShapeDtypeStruct((m, d), out_dtype),
        compiler_params=_params(("arbitrary",)),
        name="rmsnorm",
    )(x, g.reshape(1, d))


def _rope_kernel(cm_ref, sm_ref, ct_ref, st_ref):
    def table(rows, pos):
        j = lax.broadcasted_iota(jnp.int32, (rows, ROPE_HALF), 1).astype(F32)
        inv = ROPE_BASE ** (-j / ROPE_HALF)
        return pos.astype(F32) * inv

    ang = table(SEQ, lax.broadcasted_iota(jnp.int32, (SEQ, ROPE_HALF), 0))
    cm_ref[...] = jnp.cos(ang)
    sm_ref[...] = jnp.sin(ang)
    r = lax.broadcasted_iota(jnp.int32, (M_TAIL, ROPE_HALF), 0)
    t = jnp.zeros_like(r) if DEC_SEQ == 1 else lax.rem(r, DEC_SEQ)
    ang_t = table(M_TAIL, PAST_LEN + t)
    ct_ref[...] = jnp.cos(ang_t)
    st_ref[...] = jnp.sin(ang_t)


def _rope_tables():
    return pl.pallas_call(
        _rope_kernel,
        out_shape=(jax.ShapeDtypeStruct((SEQ, ROPE_HALF), F32),
                   jax.ShapeDtypeStruct((SEQ, ROPE_HALF), F32),
                   jax.ShapeDtypeStruct((M_TAIL, ROPE_HALF), F32),
                   jax.ShapeDtypeStruct((M_TAIL, ROPE_HALF), F32)),
        name="rope_tables",
    )()


CAST_ROWS = 256


def _wres_kernel(*refs, lhs_of, w_offs, n_tiles, n_ext, n_out, mt, has_tail, sub, epilogue):
    refs = list(refs)
    n_pairs, n_lhs = len(lhs_of), max(lhs_of) + 1

    def take(k):
        out = refs[:k]
        del refs[:k]
        return out

    lhs_main = take(n_lhs)
    lhs_tail = take(n_lhs) if has_tail else []
    w_hbm = take(n_pairs)
    ext_main = take(n_ext)
    ext_tail = take(n_ext) if has_tail else []
    out_main = take(n_out)
    out_tail = take(n_out) if has_tail else []
    wf = take(n_pairs)
    wb = take(n_pairs)
    (sem,) = take(1)

    n = pl.program_id(0)
    m = pl.program_id(1)
    tn = wb[0].shape[1]

    def w_copy(p, tile):
        col = pl.multiple_of(w_offs[p] + tile * tn, tn)
        return pltpu.make_async_copy(w_hbm[p].at[:, pl.ds(col, tn)], wf[p], sem.at[p])

    @pl.when(m == 0)
    def _weights():
        @pl.when(n == 0)
        def _first():
            for p in range(n_pairs):
                w_copy(p, 0).start()

        for p in range(n_pairs):
            w_copy(p, n).wait()

            def body(i, carry, wf_ref=wf[p], wb_ref=wb[p]):
                r = pl.multiple_of(i * CAST_ROWS, CAST_ROWS)
                wb_ref[pl.ds(r, CAST_ROWS), :] = wf_ref[pl.ds(r, CAST_ROWS), :].astype(BF16)
                return carry
            lax.fori_loop(0, wf[p].shape[0] // CAST_ROWS, body, 0)

        @pl.when(n + 1 < n_tiles)
        def _next():
            for p in range(n_pairs):
                w_copy(p, n + 1).start()

    def body(lhs, ext, outs):
        tn = wb[0].shape[1]
        for c in range(0, tn, sub):
            accs = [jnp.dot(lhs[i][...].astype(BF16), b[:, c:c + sub], preferred_element_type=F32)
                    for i, b in zip(lhs_of, wb)]
            res = epilogue(accs, ext, c, sub, n)
            for o, r in zip(outs, res):
                o[:, c:c + sub] = r.astype(o.dtype)

    body(lhs_main, ext_main, out_main)
    if has_tail:
        @pl.when(m == mt - 1)
        def _tail():
            body(lhs_tail, ext_tail, out_tail)


def _wres_matmul(pairs, exts, outs, *, n_cols, tn, tm, sub, epilogue, name):
    m_main = pairs[0][0].shape[0]
    has_tail = pairs[0][1] is not None
    mt = m_main // tm
    n_tiles = n_cols // tn
    grid = (n_tiles, mt)

    def mrow(m):
        return m

    lhs, lhs_of = [], []
    for lm, lt, _, _ in pairs:
        ids = [i for i, (a, _) in enumerate(lhs) if a is lm]
        if not ids:
            lhs.append((lm, lt))
        lhs_of.append(ids[0] if ids else len(lhs) - 1)

    in_specs, args = [], []
    for lm, _ in lhs:
        in_specs.append(pl.BlockSpec((tm, lm.shape[1]), lambda n, m: (mrow(m), 0)))
        args.append(lm)
    if has_tail:
        for _, lt in lhs:
            in_specs.append(pl.BlockSpec(lt.shape, lambda n, m: (0, 0)))
            args.append(lt)
    for _, _, w, off in pairs:
        assert off % tn == 0 and off + n_cols <= w.shape[1]
        in_specs.append(pl.BlockSpec(memory_space=pl.ANY))
        args.append(w)

    def ext_spec(arr, kind, off, tail):
        rows = arr.shape[0] if tail else tm
        if kind == "tile":
            assert off % tn == 0
            if tail:
                return pl.BlockSpec((rows, tn), lambda n, m, o=off // tn: (0, o + n))
            return pl.BlockSpec((rows, tn), lambda n, m, o=off // tn: (mrow(m), o + n))
        assert kind == "rope"
        if tail:
            return pl.BlockSpec(arr.shape, lambda n, m: (0, 0))
        per = arr.shape[0] // tm
        return pl.BlockSpec((tm, arr.shape[1]), lambda n, m: (lax.rem(mrow(m), per), 0))

    for em, _, kind, off in exts:
        in_specs.append(ext_spec(em, kind, off, False))
        args.append(em)
    if has_tail:
        for _, et, kind, off in exts:
            in_specs.append(ext_spec(et, kind, off, True))
            args.append(et)

    out_specs, out_shape = [], []
    for dm, _ in outs:
        out_specs.append(pl.BlockSpec((tm, tn), lambda n, m: (mrow(m), n)))
        out_shape.append(jax.ShapeDtypeStruct((m_main, n_cols), dm))
    if has_tail:
        m_tail = pairs[0][1].shape[0]
        for _, dt in outs:
            out_specs.append(pl.BlockSpec((m_tail, tn), lambda n, m: (0, n)))
            out_shape.append(jax.ShapeDtypeStruct((m_tail, n_cols), dt))

    scratch = ([pltpu.VMEM((w.shape[0], tn), F32) for _, _, w, _ in pairs]
               + [pltpu.VMEM((w.shape[0], tn), BF16) for _, _, w, _ in pairs]
               + [pltpu.SemaphoreType.DMA((len(pairs),))])
    kern = functools.partial(_wres_kernel, lhs_of=tuple(lhs_of),
                             w_offs=tuple(off for _, _, _, off in pairs), n_tiles=n_tiles,
                             n_ext=len(exts), n_out=len(outs),
                             mt=mt, has_tail=has_tail, sub=sub, epilogue=epilogue)
    return pl.pallas_call(
        kern, grid=grid, in_specs=in_specs, out_specs=out_specs, out_shape=out_shape,
        scratch_shapes=scratch, compiler_params=_params(("arbitrary", "arbitrary")), name=name,
    )(*args)


def _epi_identity(accs, ext, c, sub, n):
    return [accs[0]]


def _epi_gelu(accs, ext, c, sub, n):
    return [jax.nn.gelu(accs[0])]


def _epi_silu(accs, ext, c, sub, n):
    return [jax.nn.silu(accs[0])]


def _epi_sigmoid(accs, ext, c, sub, n):
    return [jax.nn.sigmoid(accs[0])]


def _epi_gelu_silu(accs, ext, c, sub, n):
    return [jax.nn.gelu(accs[0]) * jax.nn.silu(accs[1])]


def _epi_rope(scale, accs, ext, c, sub, n):
    assert sub == RET_HEAD_DIM
    x = accs[0]
    x1, x2 = x[:, :ROPE_HALF], x[:, ROPE_HALF:]
    cos, sin = ext[0][...], ext[1][...]
    out = jnp.concatenate([x1 * cos - x2 * sin, x1 * sin + x2 * cos], axis=-1)
    return [out if scale == 1.0 else out * scale]


def _epi_merge(accs, ext, c, sub, n):
    g0, g1, g2 = (e[:, c:c + sub] for e in ext)
    return [g0 * accs[0] + g1 * accs[1] + g2 * accs[2]]


def _epi_residual(accs, ext, c, sub, n):
    return [ext[0][:, c:c + sub] + accs[0]]


def _layer_norm(x, g):
    mu = jnp.mean(x, axis=-1, keepdims=True)
    var = jnp.mean(jnp.square(x - mu), axis=-1, keepdims=True)
    return (x - mu) * lax.rsqrt(var + EPS) * g


def _spatial_main_kernel(ug_ref, gv_ref, ga_ref, ws_ref, bst_ref, o_ref):
    rows = gv_ref.shape[0]
    gw = A_WIDTH // A_GROUPS
    vn = _layer_norm(gv_ref[...].astype(F32), ga_ref[...]).astype(BF16)
    ri = lax.broadcasted_iota(jnp.int32, (CHUNK, CHUNK), 0)
    ci = lax.broadcasted_iota(jnp.int32, (CHUNK, CHUNK), 1)
    for g in range(A_GROUPS):
        w = jnp.where(ri >= ci, ws_ref[g], 0.0).astype(BF16)
        bias = bst_ref[:, g:g + 1]
        for c in range(rows // CHUNK):
            rs = slice(c * CHUNK, (c + 1) * CHUNK)
            cs = slice(g * gw, (g + 1) * gw)
            sp = jnp.dot(w, vn[rs, cs], preferred_element_type=F32) + bias
            o_ref[rs, cs] = (ug_ref[rs, cs] * sp).astype(o_ref.dtype)


def _spatial_main(ug, gv, g_anorm, w_s, b_s, rows=512):
    m = ug.shape[0]
    return pl.pallas_call(
        _spatial_main_kernel,
        grid=(m // rows,),
        in_specs=[pl.BlockSpec((rows, A_WIDTH), lambda i: (i, 0)),
                  pl.BlockSpec((rows, A_WIDTH), lambda i: (i, 0)),
                  pl.BlockSpec((1, A_WIDTH), lambda i: (0, 0)),
                  pl.BlockSpec((A_GROUPS, CHUNK, CHUNK), lambda i: (0, 0, 0)),
                  pl.BlockSpec((CHUNK, A_GROUPS), lambda i: (0, 0))],
        out_specs=pl.BlockSpec((rows, A_WIDTH), lambda i: (i, 0)),
        out_shape=jax.ShapeDtypeStruct((m, A_WIDTH), BF16),
        compiler_params=_params(("arbitrary",)),
        name="spatial_main",
    )(ug, gv, g_anorm.reshape(1, A_WIDTH), w_s, b_s.T)


def _spatial_tail_kernel(ug_ref, gv_ref, ga_ref, ws_ref, bs_ref, o_ref, vn_ref):
    gw = A_WIDTH // A_GROUPS
    vn = _layer_norm(gv_ref[...], ga_ref[...])
    vn_ref[...] = vn
    for g in range(A_GROUPS):
        cs = slice(g * gw, (g + 1) * gw)
        sp = vn[:, cs] * ws_ref[g, 0:1, 0:1] + bs_ref[g:g + 1, 0:1]
        o_ref[:, cs] = (ug_ref[:, cs] * sp).astype(o_ref.dtype)


def _spatial_tail(ug, gv, g_anorm, w_s, b_s):
    m = ug.shape[0]
    return pl.pallas_call(
        _spatial_tail_kernel,
        out_shape=(jax.ShapeDtypeStruct((m, A_WIDTH), BF16),
                   jax.ShapeDtypeStruct((m, A_WIDTH), F32)),
        compiler_params=_params(None),
        name="spatial_tail",
    )(ug, gv, g_anorm.reshape(1, A_WIDTH), w_s, b_s)


def _log_gamma(shape, head):
    return jnp.log1p(-jnp.exp2(jnp.full(shape, -5.0, F32) - head.astype(F32)))


def _group_norm(o, g):
    mu = jnp.mean(o, axis=-1, keepdims=True)
    var = jnp.mean(jnp.square(o - mu), axis=-1, keepdims=True)
    return (o - mu) * lax.rsqrt(var + EPS) * g


def _ret_main_kernel(q_ref, k_ref, v_ref, sg_ref, gr_ref, o_ref, r_ref, state):
    head = pl.program_id(1)
    L, dk = CHUNK, RET_HEAD_DIM
    ri = lax.broadcasted_iota(jnp.int32, (L, L), 0).astype(F32)
    ci = lax.broadcasted_iota(jnp.int32, (L, L), 1).astype(F32)
    diff = ri - ci
    decay_in = jnp.where(diff >= 0, jnp.exp(_log_gamma((L, L), head) * jnp.maximum(diff, 0.0)), 0.0)
    rw = lax.broadcasted_iota(jnp.int32, (L, dk), 0).astype(F32)
    lg_w = _log_gamma((L, dk), head)
    decay_q = jnp.exp(lg_w * (rw + 1.0))
    decay_k = jnp.exp(lg_w * (L - 1.0 - rw))
    decay_blk = jnp.exp(_log_gamma((dk, dk), head) * L)
    gr = gr_ref[...]
    state[...] = jnp.zeros_like(state)

    def body(i, carry):
        r0 = pl.multiple_of(i * L, L)
        qi = q_ref[pl.ds(r0, L), :]
        ki = k_ref[pl.ds(r0, L), :]
        vi = v_ref[pl.ds(r0, L), :]
        s = lax.dot_general(qi, ki, (((1,), (1,)), ((), ())), preferred_element_type=F32) * decay_in
        R = state[...]
        o = (jnp.dot(s.astype(BF16), vi, preferred_element_type=F32)
             + jnp.dot(qi, R.astype(BF16), preferred_element_type=F32) * decay_q)
        kd = (ki.astype(F32) * decay_k).astype(BF16)
        state[...] = R * decay_blk + lax.dot_general(
            kd, vi, (((0,), (0,)), ((), ())), preferred_element_type=F32)
        o_ref[pl.ds(r0, L), :] = (_group_norm(o, gr) * sg_ref[pl.ds(r0, L), :]).astype(o_ref.dtype)
        return carry

    lax.fori_loop(0, SEQ // L, body, 0, unroll=True)
    r_ref[...] = state[...]


def _ret_main(q, k, v, sg, g_ret):
    dk = RET_HEAD_DIM
    blk = lambda: pl.BlockSpec((SEQ, dk), lambda b, h: (b, h))
    return pl.pallas_call(
        _ret_main_kernel,
        grid=(BATCH, RET_HEADS),
        in_specs=[blk(), blk(), blk(), blk(), pl.BlockSpec((1, dk), lambda b, h: (0, h))],
        out_specs=(blk(), pl.BlockSpec((None, None, None, dk, dk), lambda b, h: (0, b, h, 0, 0))),
        out_shape=(jax.ShapeDtypeStruct((M_MAIN, RET_WIDTH), BF16),
                   jax.ShapeDtypeStruct((DEPTH, BATCH, RET_HEADS, dk, dk), F32)),
        scratch_shapes=[pltpu.VMEM((dk, dk), F32)],
        compiler_params=_params(("arbitrary", "arbitrary")),
        name="ret_main",
    )(q, k, v, sg, g_ret.reshape(1, RET_WIDTH))


RET_TAIL_ROWS = 16


def _ret_tail_kernel(q_ref, k_ref, v_ref, sg_ref, gr_ref, s_ref, o_ref, so_ref):
    head = pl.program_id(1)
    nb, dk = q_ref.shape
    gam_row = jnp.exp(_log_gamma((nb, dk), head))
    gam_st = jnp.exp(_log_gamma((dk, dk), head))
    q, k, v = q_ref[...], k_ref[...], v_ref[...]
    qb, vb = q.astype(BF16), v.astype(BF16)
    rows = lax.broadcasted_iota(jnp.int32, (nb, dk), 0)
    qr = jnp.zeros((nb, dk), F32)
    for r in range(nb):
        R = s_ref[r]
        qr = jnp.where(rows == r, jnp.dot(qb, R.astype(BF16), preferred_element_type=F32), qr)
        k_r = jnp.where(rows == r, k, 0.0).astype(BF16)
        so_ref[r] = R * gam_st + lax.dot_general(
            k_r, vb, (((0,), (0,)), ((), ())), preferred_element_type=F32)
    qk = jnp.sum(q * k, axis=-1, keepdims=True)
    o = qk * v + qr * gam_row
    o_ref[...] = _group_norm(o, gr_ref[...]) * sg_ref[...]


def _ret_tail(q, k, v, sg, g_ret, state):
    dk, nb = RET_HEAD_DIM, RET_TAIL_ROWS
    row = lambda: pl.BlockSpec((nb, dk), lambda i, h: (i, h))
    st = lambda: pl.BlockSpec((None, nb, None, dk, dk), lambda i, h: (0, i, h, 0, 0))
    return pl.pallas_call(
        _ret_tail_kernel,
        grid=(DEC_BATCH // nb, RET_HEADS),
        in_specs=[row(), row(), row(), row(), pl.BlockSpec((1, dk), lambda i, h: (0, h)), st()],
        out_specs=(row(), st()),
        out_shape=(jax.ShapeDtypeStruct((M_TAIL, RET_WIDTH), F32),
                   jax.ShapeDtypeStruct(state.shape, F32)),
        compiler_params=_params(("arbitrary", "arbitrary")),
        name="ret_tail",
    )(q, k, v, sg, g_ret.reshape(1, RET_WIDTH), state)


XATTN_ROWS = 512


def _xattn_main_kernel(q_ref, k_ref, v_ref, g_ref, o_ref):
    k = k_ref[...].astype(BF16)
    v = v_ref[...].astype(BF16)
    for t in range(q_ref.shape[0] // XATTN_ROWS):
        rs = slice(t * XATTN_ROWS, (t + 1) * XATTN_ROWS)
        sc = lax.dot_general(q_ref[rs, :], k, (((1,), (1,)), ((), ())), preferred_element_type=F32)
        sc = sc * (MEM_HEAD_DIM ** -0.5)
        e = jnp.exp(sc - jnp.max(sc, axis=-1, keepdims=True))
        p = e / jnp.sum(e, axis=-1, keepdims=True)
        om = jnp.dot(p.astype(BF16), v, preferred_element_type=F32)
        o_ref[rs, :] = (om * g_ref[rs, :]).astype(o_ref.dtype)


def _xattn_main(cq, mk, mv, scg):
    dh = MEM_HEAD_DIM
    qspec = lambda: pl.BlockSpec((SEQ, dh), lambda b, h: (b, h))
    kvspec = lambda: pl.BlockSpec((MEM_LEN, dh), lambda b, h: (b, h))
    return pl.pallas_call(
        _xattn_main_kernel,
        grid=(BATCH, MEM_HEADS),
        in_specs=[qspec(), kvspec(), kvspec(), qspec()],
        out_specs=qspec(),
        out_shape=jax.ShapeDtypeStruct((M_MAIN, MEM_WIDTH), BF16),
        compiler_params=_params(("arbitrary", "arbitrary")),
        name="xattn_main",
    )(cq, mk, mv, scg)


XATTN_TAIL_ROWS = 4


def _xattn_tail_kernel(q_ref, g_ref, k_ref, v_ref, o_ref):
    for r in range(q_ref.shape[0]):
        q = q_ref[r]
        sc = jnp.sum(k_ref[r] * q[None], axis=-1, keepdims=True) * (MEM_HEAD_DIM ** -0.5)
        e = jnp.exp(sc - jnp.max(sc, axis=0, keepdims=True))
        p = e / jnp.sum(e, axis=0, keepdims=True)
        o_ref[r] = jnp.sum(p * v_ref[r], axis=0) * g_ref[r]


def _xattn_tail(cq, scg, ck, cv):
    nb = XATTN_TAIL_ROWS
    row = lambda: pl.BlockSpec((nb, MEM_HEADS, MEM_HEAD_DIM), lambda i: (i, 0, 0))
    kv = lambda: pl.BlockSpec((None, nb, MEM_LEN, MEM_HEADS, MEM_HEAD_DIM), lambda i: (0, i, 0, 0, 0))
    shape3 = (M_TAIL, MEM_HEADS, MEM_HEAD_DIM)
    out = pl.pallas_call(
        _xattn_tail_kernel,
        grid=(DEC_BATCH // nb,),
        in_specs=[row(), row(), kv(), kv()],
        out_specs=row(),
        out_shape=jax.ShapeDtypeStruct(shape3, F32),
        compiler_params=_params(("arbitrary",)),
        name="xattn_tail",
    )(cq.reshape(shape3), scg.reshape(shape3), ck, cv)
    return out.reshape(M_TAIL, MEM_WIDTH)


def kernel(x_prompt, x_sample, state_ret, cache_mem_k, cache_mem_v, mem_prompt, g_pre, w_in, g_anorm,
           w_s, b_s, g_ret, g_mem, w_mem_kv, w_out_a, w_out_b, w_out_c, w_out, g_final):
    assert DEPTH == 1 and w_in.shape == (DEPTH, D_MODEL, IN_WIDTH)
    xp = x_prompt.reshape(M_MAIN, D_MODEL)
    xs = x_sample.reshape(M_TAIL, D_MODEL)
    win = w_in.reshape(D_MODEL, IN_WIDTH)

    h_m = _rmsnorm(xp, g_pre[0], 512, BF16)
    h_t = _rmsnorm(xs, g_pre[0], M_TAIL, BF16)
    cos_m, sin_m, cos_t, sin_t = _rope_tables()

    def inproj(offs, n_cols, epilogue, out_dtypes, name, tn=1024, exts=()):
        return _wres_matmul([(h_m, h_t, win, o) for o in offs], list(exts), [out_dtypes],
                            n_cols=n_cols, tn=tn, tm=1024, sub=256, epilogue=epilogue, name=name)

    rope = [(cos_m, cos_t, "rope", 0), (sin_m, sin_t, "rope", 0)]
    ug_m, ug_t = inproj([OFF_AU, OFF_AG], A_WIDTH, _epi_gelu_silu, (BF16, F32), "inproj_ug", tn=512)
    gv_m, gv_t = inproj([OFF_AV], A_WIDTH, _epi_gelu, (BF16, F32), "inproj_gv")
    q_m, q_t = inproj([OFF_RQ], RET_WIDTH, functools.partial(_epi_rope, 1.0), (BF16, F32),
                      "inproj_q", exts=rope)
    k_m, k_t = inproj([OFF_RK], RET_WIDTH, functools.partial(_epi_rope, RET_HEAD_DIM ** -0.5),
                      (BF16, F32), "inproj_k", exts=rope)
    v_m, v_t = inproj([OFF_RV], RET_WIDTH, _epi_identity, (BF16, F32), "inproj_v")
    sg_m, sg_t = inproj([OFF_RG], RET_WIDTH, _epi_silu, (BF16, F32), "inproj_rg")
    cq_m, cq_t = inproj([OFF_CQ], MEM_WIDTH, _epi_identity, (BF16, F32), "inproj_cq")
    scg_m, scg_t = inproj([OFF_CG], MEM_WIDTH, _epi_silu, (BF16, F32), "inproj_cg")
    gs_m, gs_t = inproj([OFF_GATES], N_BRANCH * D_MODEL, _epi_sigmoid, (BF16, BF16), "inproj_gates")

    a_m = _spatial_main(ug_m, gv_m, g_anorm[0], w_s[0], b_s[0])
    a_t, vn_t = _spatial_tail(ug_t, gv_t, g_anorm[0], w_s[0], b_s[0])

    b_m, ret_p = _ret_main(q_m, k_m, v_m, sg_m, g_ret[0])
    b_t, ret_s = _ret_tail(q_t, k_t, v_t, sg_t, g_ret[0], state_ret)

    hm = _rmsnorm(mem_prompt.reshape(BATCH * MEM_LEN, D_MODEL), g_mem[0], 512, BF16)
    wkv = w_mem_kv.reshape(D_MODEL, 2 * MEM_WIDTH)
    (mk,) = _wres_matmul([(hm, None, wkv, 0)], [], [(F32, None)], n_cols=MEM_WIDTH, tn=512, tm=512,
                         sub=256, epilogue=_epi_identity, name="mem_k")
    (mv,) = _wres_matmul([(hm, None, wkv, MEM_WIDTH)], [], [(F32, None)], n_cols=MEM_WIDTH, tn=512,
                         tm=512, sub=256, epilogue=_epi_identity, name="mem_v")
    c_m = _xattn_main(cq_m, mk, mv, scg_m)
    c_t = _xattn_tail(cq_t, scg_t, cache_mem_k, cache_mem_v)

    merged_m, merged_t = _wres_matmul(
        [(a_m, a_t, w_out_a.reshape(A_WIDTH, D_MODEL), 0),
         (b_m, b_t, w_out_b.reshape(RET_WIDTH, D_MODEL), 0),
         (c_m, c_t, w_out_c.reshape(MEM_WIDTH, D_MODEL), 0)],
        [(gs_m, gs_t, "tile", b * D_MODEL) for b in range(N_BRANCH)],
        [(BF16, BF16)], n_cols=D_MODEL, tn=512, tm=1024, sub=256, epilogue=_epi_merge, name="merge")
    y_m, y_t = _wres_matmul(
        [(merged_m, merged_t, w_out.reshape(D_MODEL, D_MODEL), 0)],
        [(xp, xs, "tile", 0)], [(F32, F32)], n_cols=D_MODEL, tn=512, tm=1024, sub=256,
        epilogue=_epi_residual, name="outproj")
    y_prompt = _rmsnorm(y_m, g_final, 512, F32).reshape(BATCH, SEQ, D_MODEL)
    y_sample = _rmsnorm(y_t, g_final, M_TAIL, F32).reshape(DEC_BATCH, DEC_SEQ, D_MODEL)

    return (y_prompt, y_sample, ret_p,
            mk.reshape(DEPTH, BATCH, MEM_LEN, MEM_HEADS, MEM_HEAD_DIM),
            mv.reshape(DEPTH, BATCH, MEM_LEN, MEM_HEADS, MEM_HEAD_DIM),
            ret_s,
            vn_t.reshape(DEPTH, DEC_BATCH, DEC_SEQ, A_WIDTH))
```

```python
import functools

import jax
import jax.numpy as jnp
from jax import lax
from jax.experimental import pallas as pl
from jax.experimental.pallas import tpu as pltpu

D_MODEL = 4096
BATCH = 4
SEQ = 2048
DEPTH = 1
DEC_BATCH = 128
DEC_SEQ = 1
PAST_LEN = 16384

CHUNK = 128
A_WIDTH = 2048
A_GROUPS = 4
RET_HEADS = 8
RET_HEAD_DIM = 256
RET_WIDTH = RET_HEADS * RET_HEAD_DIM
MEM_LEN = 256
MEM_HEADS = 4
MEM_HEAD_DIM = 256
MEM_WIDTH = MEM_HEADS * MEM_HEAD_DIM
N_BRANCH = 3
ROPE_BASE = 10000.0
EPS = 1e-6

OFF_AU = 0
OFF_AV = OFF_AU + A_WIDTH
OFF_AG = OFF_AV + A_WIDTH
OFF_RQ = OFF_AG + A_WIDTH
OFF_RK = OFF_RQ + RET_WIDTH
OFF_RV = OFF_RK + RET_WIDTH
OFF_RG = OFF_RV + RET_WIDTH
OFF_CQ = OFF_RG + RET_WIDTH
OFF_CG = OFF_CQ + MEM_WIDTH
OFF_GATES = OFF_CG + MEM_WIDTH
IN_WIDTH = OFF_GATES + N_BRANCH * D_MODEL

M_MAIN = BATCH * SEQ
M_TAIL = DEC_BATCH * DEC_SEQ
ROPE_HALF = RET_HEAD_DIM // 2

F32 = jnp.float32
BF16 = jnp.bfloat16

MIB = 1024 * 1024
VMEM_BUDGET_BYTES = 56 * MIB


def _params(semantics, vmem_bytes=VMEM_BUDGET_BYTES):
    return pltpu.CompilerParams(dimension_semantics=semantics, vmem_limit_bytes=vmem_bytes)


def _rmsnorm_kernel(x_ref, g_ref, o_ref):
    x = x_ref[...].astype(F32)
    y = x * lax.rsqrt(jnp.mean(x * x, axis=-1, keepdims=True) + EPS)
    o_ref[...] = (y * g_ref[...].astype(F32)).astype(o_ref.dtype)


def _rmsnorm(x, g, rows, out_dtype):
    m, d = x.shape
    return pl.pallas_call(
        _rmsnorm_kernel,
        grid=(m // rows,),
        in_specs=[pl.BlockSpec((rows, d), lambda i: (i, 0)),
                  pl.BlockSpec((1, d), lambda i: (0, 0))],
        out_specs=pl.BlockSpec((rows, d), lambda i: (i, 0)),
        out_shape=jax.ShapeDtypeStruct((m, d), out_dtype),
        compiler_params=_params(("arbitrary",)),
        name="rmsnorm",
    )(x, g.reshape(1, d))


def _rope_kernel(cm_ref, sm_ref, ct_ref, st_ref):
    def table(rows, pos):
        j = lax.broadcasted_iota(jnp.int32, (rows, ROPE_HALF), 1).astype(F32)
        inv = ROPE_BASE ** (-j / ROPE_HALF)
        return pos.astype(F32) * inv

    ang = table(SEQ, lax.broadcasted_iota(jnp.int32, (SEQ, ROPE_HALF), 0))
    cm_ref[...] = jnp.cos(ang)
    sm_ref[...] = jnp.sin(ang)
    r = lax.broadcasted_iota(jnp.int32, (M_TAIL, ROPE_HALF), 0)
    t = jnp.zeros_like(r) if DEC_SEQ == 1 else lax.rem(r, DEC_SEQ)
    ang_t = table(M_TAIL, PAST_LEN + t)
    ct_ref[...] = jnp.cos(ang_t)
    st_ref[...] = jnp.sin(ang_t)


def _rope_tables():
    return pl.pallas_call(
        _rope_kernel,
        out_shape=(jax.ShapeDtypeStruct((SEQ, ROPE_HALF), F32),
                   jax.ShapeDtypeStruct((SEQ, ROPE_HALF), F32),
                   jax.ShapeDtypeStruct((M_TAIL, ROPE_HALF), F32),
                   jax.ShapeDtypeStruct((M_TAIL, ROPE_HALF), F32)),
        name="rope_tables",
    )()


CAST_ROWS = 256


def _wres_kernel(*refs, lhs_of, w_offs, n_tiles, n_ext, n_out, mt, has_tail, sub, epilogue,
                 side_fn, n_side_in, n_side_out):
    refs = list(refs)
    n_pairs, n_lhs = len(lhs_of), max(lhs_of) + 1

    def take(k):
        out = refs[:k]
        del refs[:k]
        return out

    lhs_main = take(n_lhs)
    lhs_tail = take(n_lhs) if has_tail else []
    w_hbm = take(n_pairs)
    ext_main = take(n_ext)
    ext_tail = take(n_ext) if has_tail else []
    side_in = take(n_side_in)
    out_main = take(n_out)
    out_tail = take(n_out) if has_tail else []
    side_out = take(n_side_out)
    wf = take(n_pairs)
    wb = take(n_pairs)
    (sem,) = take(1)

    n = pl.program_id(0)
    m = pl.program_id(1)
    tn = wb[0].shape[1]

    def w_copy(p, tile):
        col = pl.multiple_of(w_offs[p] + tile * tn, tn)
        return pltpu.make_async_copy(w_hbm[p].at[:, pl.ds(col, tn)], wf[p], sem.at[p])

    @pl.when(m == 0)
    def _weights():
        @pl.when(n == 0)
        def _first():
            for p in range(n_pairs):
                w_copy(p, 0).start()

        for p in range(n_pairs):
            w_copy(p, n).wait()

            def body(i, carry, wf_ref=wf[p], wb_ref=wb[p]):
                r = pl.multiple_of(i * CAST_ROWS, CAST_ROWS)
                wb_ref[pl.ds(r, CAST_ROWS), :] = wf_ref[pl.ds(r, CAST_ROWS), :].astype(BF16)
                return carry
            lax.fori_loop(0, wf[p].shape[0] // CAST_ROWS, body, 0)

        @pl.when(n + 1 < n_tiles)
        def _next():
            for p in range(n_pairs):
                w_copy(p, n + 1).start()

    def body(lhs, ext, outs):
        tn = wb[0].shape[1]
        for c in range(0, tn, sub):
            accs = [jnp.dot(lhs[i][...].astype(BF16), b[:, c:c + sub], preferred_element_type=F32)
                    for i, b in zip(lhs_of, wb)]
            res = epilogue(accs, ext, c, sub, n)
            for o, r in zip(outs, res):
                o[:, c:c + sub] = r.astype(o.dtype)

    body(lhs_main, ext_main, out_main)
    if side_fn is not None:
        side_fn(n, m, *side_in, *side_out)
    if has_tail:
        @pl.when(m == mt - 1)
        def _tail():
            body(lhs_tail, ext_tail, out_tail)


def _wres_matmul(pairs, exts, outs, *, n_cols, tn, tm, sub, epilogue, name, side=None,
                 vmem_bytes=VMEM_BUDGET_BYTES):
    m_main = pairs[0][0].shape[0]
    has_tail = pairs[0][1] is not None
    mt = m_main // tm
    n_tiles = n_cols // tn
    grid = (n_tiles, mt)

    def mrow(m):
        return m

    lhs, lhs_of = [], []
    for lm, lt, _, _ in pairs:
        ids = [i for i, (a, _) in enumerate(lhs) if a is lm]
        if not ids:
            lhs.append((lm, lt))
        lhs_of.append(ids[0] if ids else len(lhs) - 1)

    in_specs, args = [], []
    for lm, _ in lhs:
        in_specs.append(pl.BlockSpec((tm, lm.shape[1]), lambda n, m: (mrow(m), 0)))
        args.append(lm)
    if has_tail:
        for _, lt in lhs:
            in_specs.append(pl.BlockSpec(lt.shape, lambda n, m: (0, 0)))
            args.append(lt)
    for _, _, w, off in pairs:
        assert off % tn == 0 and off + n_cols <= w.shape[1]
        in_specs.append(pl.BlockSpec(memory_space=pl.ANY))
        args.append(w)

    def ext_spec(arr, kind, off, tail):
        rows = arr.shape[0] if tail else tm
        if kind == "tile":
            assert off % tn == 0
            if tail:
                return pl.BlockSpec((rows, tn), lambda n, m, o=off // tn: (0, o + n))
            return pl.BlockSpec((rows, tn), lambda n, m, o=off // tn: (mrow(m), o + n))
        assert kind == "rope"
        if tail:
            return pl.BlockSpec(arr.shape, lambda n, m: (0, 0))
        per = arr.shape[0] // tm
        return pl.BlockSpec((tm, arr.shape[1]), lambda n, m: (lax.rem(mrow(m), per), 0))

    for em, _, kind, off in exts:
        in_specs.append(ext_spec(em, kind, off, False))
        args.append(em)
    if has_tail:
        for _, et, kind, off in exts:
            in_specs.append(ext_spec(et, kind, off, True))
            args.append(et)

    out_specs, out_shape = [], []
    for dm, _ in outs:
        out_specs.append(pl.BlockSpec((tm, tn), lambda n, m: (mrow(m), n)))
        out_shape.append(jax.ShapeDtypeStruct((m_main, n_cols), dm))
    if has_tail:
        m_tail = pairs[0][1].shape[0]
        for _, dt in outs:
            out_specs.append(pl.BlockSpec((m_tail, tn), lambda n, m: (0, n)))
            out_shape.append(jax.ShapeDtypeStruct((m_tail, n_cols), dt))

    if side is not None:
        assert n_tiles * mt >= side["min_steps"]
        in_specs += side["in_specs"]
        args += side["args"]
        out_specs += side["out_specs"]
        out_shape += side["out_shape"]

    scratch = ([pltpu.VMEM((w.shape[0], tn), F32) for _, _, w, _ in pairs]
               + [pltpu.VMEM((w.shape[0], tn), BF16) for _, _, w, _ in pairs]
               + [pltpu.SemaphoreType.DMA((len(pairs),))])
    kern = functools.partial(_wres_kernel, lhs_of=tuple(lhs_of),
                             w_offs=tuple(off for _, _, _, off in pairs), n_tiles=n_tiles,
                             n_ext=len(exts), n_out=len(outs),
                             mt=mt, has_tail=has_tail, sub=sub, epilogue=epilogue,
                             side_fn=side["fn"] if side else None,
                             n_side_in=len(side["args"]) if side else 0,
                             n_side_out=len(side["out_shape"]) if side else 0)
    return pl.pallas_call(
        kern, grid=grid, in_specs=in_specs, out_specs=out_specs, out_shape=out_shape,
        scratch_shapes=scratch, compiler_params=_params(("arbitrary", "arbitrary"), vmem_bytes),
        name=name,
    )(*args)


def _epi_identity(accs, ext, c, sub, n):
    return [accs[0]]


def _epi_gelu(accs, ext, c, sub, n):
    return [jax.nn.gelu(accs[0])]


def _epi_silu(accs, ext, c, sub, n):
    return [jax.nn.silu(accs[0])]


def _epi_sigmoid(accs, ext, c, sub, n):
    return [jax.nn.sigmoid(accs[0])]


def _epi_gelu_silu(accs, ext, c, sub, n):
    return [jax.nn.gelu(accs[0]) * jax.nn.silu(accs[1])]


def _epi_rope(scale, accs, ext, c, sub, n):
    assert sub == RET_HEAD_DIM
    x = accs[0]
    x1, x2 = x[:, :ROPE_HALF], x[:, ROPE_HALF:]
    cos, sin = ext[0][...], ext[1][...]
    out = jnp.concatenate([x1 * cos - x2 * sin, x1 * sin + x2 * cos], axis=-1)
    return [out if scale == 1.0 else out * scale]


def _epi_merge(accs, ext, c, sub, n):
    g0, g1, g2 = (e[:, c:c + sub] for e in ext)
    return [g0 * accs[0] + g1 * accs[1] + g2 * accs[2]]


def _epi_residual(accs, ext, c, sub, n):
    return [ext[0][:, c:c + sub] + accs[0]]


def _layer_norm(x, g):
    mu = jnp.mean(x, axis=-1, keepdims=True)
    var = jnp.mean(jnp.square(x - mu), axis=-1, keepdims=True)
    return (x - mu) * lax.rsqrt(var + EPS) * g


def _spatial_main_kernel(ug_ref, gv_ref, ga_ref, ws_ref, bst_ref, o_ref):
    rows = gv_ref.shape[0]
    gw = A_WIDTH // A_GROUPS
    vn = _layer_norm(gv_ref[...].astype(F32), ga_ref[...]).astype(BF16)
    ri = lax.broadcasted_iota(jnp.int32, (CHUNK, CHUNK), 0)
    ci = lax.broadcasted_iota(jnp.int32, (CHUNK, CHUNK), 1)
    for g in range(A_GROUPS):
        w = jnp.where(ri >= ci, ws_ref[g], 0.0).astype(BF16)
        bias = bst_ref[:, g:g + 1]
        for c in range(rows // CHUNK):
            rs = slice(c * CHUNK, (c + 1) * CHUNK)
            cs = slice(g * gw, (g + 1) * gw)
            sp = jnp.dot(w, vn[rs, cs], preferred_element_type=F32) + bias
            o_ref[rs, cs] = (ug_ref[rs, cs] * sp).astype(o_ref.dtype)


def _spatial_main(ug, gv, g_anorm, w_s, b_s, rows=512):
    m = ug.shape[0]
    return pl.pallas_call(
        _spatial_main_kernel,
        grid=(m // rows,),
        in_specs=[pl.BlockSpec((rows, A_WIDTH), lambda i: (i, 0)),
                  pl.BlockSpec((rows, A_WIDTH), lambda i: (i, 0)),
                  pl.BlockSpec((1, A_WIDTH), lambda i: (0, 0)),
                  pl.BlockSpec((A_GROUPS, CHUNK, CHUNK), lambda i: (0, 0, 0)),
                  pl.BlockSpec((CHUNK, A_GROUPS), lambda i: (0, 0))],
        out_specs=pl.BlockSpec((rows, A_WIDTH), lambda i: (i, 0)),
        out_shape=jax.ShapeDtypeStruct((m, A_WIDTH), BF16),
        compiler_params=_params(("arbitrary",)),
        name="spatial_main",
    )(ug, gv, g_anorm.reshape(1, A_WIDTH), w_s, b_s.T)


def _spatial_tail_kernel(ug_ref, gv_ref, ga_ref, ws_ref, bs_ref, o_ref, vn_ref):
    gw = A_WIDTH // A_GROUPS
    vn = _layer_norm(gv_ref[...], ga_ref[...])
    vn_ref[...] = vn
    for g in range(A_GROUPS):
        cs = slice(g * gw, (g + 1) * gw)
        sp = vn[:, cs] * ws_ref[g, 0:1, 0:1] + bs_ref[g:g + 1, 0:1]
        o_ref[:, cs] = (ug_ref[:, cs] * sp).astype(o_ref.dtype)


def _spatial_tail(ug, gv, g_anorm, w_s, b_s):
    m = ug.shape[0]
    return pl.pallas_call(
        _spatial_tail_kernel,
        out_shape=(jax.ShapeDtypeStruct((m, A_WIDTH), BF16),
                   jax.ShapeDtypeStruct((m, A_WIDTH), F32)),
        compiler_params=_params(None),
        name="spatial_tail",
    )(ug, gv, g_anorm.reshape(1, A_WIDTH), w_s, b_s)


def _log_gamma(shape, head):
    return jnp.log1p(-jnp.exp2(jnp.full(shape, -5.0, F32) - head.astype(F32)))


def _group_norm(o, g):
    mu = jnp.mean(o, axis=-1, keepdims=True)
    var = jnp.mean(jnp.square(o - mu), axis=-1, keepdims=True)
    return (o - mu) * lax.rsqrt(var + EPS) * g


def _ret_main_kernel(q_ref, k_ref, v_ref, sg_ref, gr_ref, o_ref, r_ref, state):
    head = pl.program_id(1)
    L, dk = CHUNK, RET_HEAD_DIM
    ri = lax.broadcasted_iota(jnp.int32, (L, L), 0).astype(F32)
    ci = lax.broadcasted_iota(jnp.int32, (L, L), 1).astype(F32)
    diff = ri - ci
    decay_in = jnp.where(diff >= 0, jnp.exp(_log_gamma((L, L), head) * jnp.maximum(diff, 0.0)), 0.0)
    rw = lax.broadcasted_iota(jnp.int32, (L, dk), 0).astype(F32)
    lg_w = _log_gamma((L, dk), head)
    decay_q = jnp.exp(lg_w * (rw + 1.0))
    decay_k = jnp.exp(lg_w * (L - 1.0 - rw))
    decay_blk = jnp.exp(_log_gamma((dk, dk), head) * L)
    gr = gr_ref[...]
    state[...] = jnp.zeros_like(state)

    def body(i, carry):
        r0 = pl.multiple_of(i * L, L)
        qi = q_ref[pl.ds(r0, L), :]
        ki = k_ref[pl.ds(r0, L), :]
        vi = v_ref[pl.ds(r0, L), :]
        s = lax.dot_general(qi, ki, (((1,), (1,)), ((), ())), preferred_element_type=F32) * decay_in
        R = state[...]
        o = (jnp.dot(s.astype(BF16), vi, preferred_element_type=F32)
             + jnp.dot(qi, R.astype(BF16), preferred_element_type=F32) * decay_q)
        kd = (ki.astype(F32) * decay_k).astype(BF16)
        state[...] = R * decay_blk + lax.dot_general(
            kd, vi, (((0,), (0,)), ((), ())), preferred_element_type=F32)
        o_ref[pl.ds(r0, L), :] = (_group_norm(o, gr) * sg_ref[pl.ds(r0, L), :]).astype(o_ref.dtype)
        return carry

    lax.fori_loop(0, SEQ // L, body, 0, unroll=True)
    r_ref[...] = state[...]


def _ret_main(q, k, v, sg, g_ret):
    dk = RET_HEAD_DIM
    blk = lambda: pl.BlockSpec((SEQ, dk), lambda b, h: (b, h))
    return pl.pallas_call(
        _ret_main_kernel,
        grid=(BATCH, RET_HEADS),
        in_specs=[blk(), blk(), blk(), blk(), pl.BlockSpec((1, dk), lambda b, h: (0, h))],
        out_specs=(blk(), pl.BlockSpec((None, None, None, dk, dk), lambda b, h: (0, b, h, 0, 0))),
        out_shape=(jax.ShapeDtypeStruct((M_MAIN, RET_WIDTH), BF16),
                   jax.ShapeDtypeStruct((DEPTH, BATCH, RET_HEADS, dk, dk), F32)),
        scratch_shapes=[pltpu.VMEM((dk, dk), F32)],
        compiler_params=_params(("arbitrary", "arbitrary")),
        name="ret_main",
    )(q, k, v, sg, g_ret.reshape(1, RET_WIDTH))


RET_TAIL_ROWS = 16


def _ret_tail_block(head, q_ref, k_ref, v_ref, sg_ref, gr_ref, s_ref, o_ref, so_ref):
    nb, dk = q_ref.shape
    gam_row = jnp.exp(_log_gamma((nb, dk), head))
    gam_st = jnp.exp(_log_gamma((dk, dk), head))
    q, k, v = q_ref[...], k_ref[...], v_ref[...]
    qt, kt = q.T, k.T
    rows = []
    for r in range(nb):
        R = s_ref[r]
        so_ref[r] = R * gam_st + kt[:, r:r + 1] * v[r:r + 1, :]
        rows.append(jnp.sum(qt[:, r:r + 1] * R, axis=0, keepdims=True))
    qr = jnp.concatenate(rows, axis=0)
    qk = jnp.sum(q * k, axis=-1, keepdims=True)
    o = qk * v + qr * gam_row
    o_ref[...] = _group_norm(o, gr_ref[...]) * sg_ref[...]


def _ret_tail_side(q, k, v, sg, g_ret, state, steps_per_tile):
    dk, nb = RET_HEAD_DIM, RET_TAIL_ROWS
    n_blocks = (DEC_BATCH // nb) * RET_HEADS

    def block(n, m):
        return jnp.minimum(n * steps_per_tile + m, n_blocks - 1)

    row = lambda: pl.BlockSpec((nb, dk), lambda n, m: (block(n, m) // RET_HEADS, block(n, m) % RET_HEADS))
    st = lambda: pl.BlockSpec((None, nb, None, dk, dk),
                              lambda n, m: (0, block(n, m) // RET_HEADS, block(n, m) % RET_HEADS, 0, 0))

    def fn(n, m, *refs):
        _ret_tail_block(block(n, m) % RET_HEADS, *refs)

    return dict(
        fn=fn, min_steps=n_blocks,
        args=[q, k, v, sg, g_ret.reshape(1, RET_WIDTH), state],
        in_specs=[row(), row(), row(), row(),
                  pl.BlockSpec((1, dk), lambda n, m: (0, block(n, m) % RET_HEADS)), st()],
        out_specs=[row(), st()],
        out_shape=[jax.ShapeDtypeStruct((M_TAIL, RET_WIDTH), F32),
                   jax.ShapeDtypeStruct(state.shape, F32)])


XATTN_ROWS = 512


def _xattn_main_kernel(q_ref, k_ref, v_ref, g_ref, o_ref):
    k = k_ref[...].astype(BF16)
    v = v_ref[...].astype(BF16)
    for t in range(q_ref.shape[0] // XATTN_ROWS):
        rs = slice(t * XATTN_ROWS, (t + 1) * XATTN_ROWS)
        sc = lax.dot_general(q_ref[rs, :], k, (((1,), (1,)), ((), ())), preferred_element_type=F32)
        sc = sc * (MEM_HEAD_DIM ** -0.5)
        e = jnp.exp(sc - jnp.max(sc, axis=-1, keepdims=True))
        p = e / jnp.sum(e, axis=-1, keepdims=True)
        om = jnp.dot(p.astype(BF16), v, preferred_element_type=F32)
        o_ref[rs, :] = (om * g_ref[rs, :]).astype(o_ref.dtype)


def _xattn_main(cq, mk, mv, scg):
    dh = MEM_HEAD_DIM
    qspec = lambda: pl.BlockSpec((SEQ, dh), lambda b, h: (b, h))
    kvspec = lambda: pl.BlockSpec((MEM_LEN, dh), lambda b, h: (b, h))
    return pl.pallas_call(
        _xattn_main_kernel,
        grid=(BATCH, MEM_HEADS),
        in_specs=[qspec(), kvspec(), kvspec(), qspec()],
        out_specs=qspec(),
        out_shape=jax.ShapeDtypeStruct((M_MAIN, MEM_WIDTH), BF16),
        compiler_params=_params(("arbitrary", "arbitrary")),
        name="xattn_main",
    )(cq, mk, mv, scg)


XATTN_TAIL_ROWS = 4


def _xattn_tail_kernel(q_ref, g_ref, k_ref, v_ref, o_ref):
    for r in range(q_ref.shape[0]):
        q = q_ref[r]
        sc = jnp.sum(k_ref[r] * q[None], axis=-1, keepdims=True) * (MEM_HEAD_DIM ** -0.5)
        e = jnp.exp(sc - jnp.max(sc, axis=0, keepdims=True))
        p = e / jnp.sum(e, axis=0, keepdims=True)
        o_ref[r] = jnp.sum(p * v_ref[r], axis=0) * g_ref[r]


def _xattn_tail(cq, scg, ck, cv):
    nb = XATTN_TAIL_ROWS
    row = lambda: pl.BlockSpec((nb, MEM_HEADS, MEM_HEAD_DIM), lambda i: (i, 0, 0))
    kv = lambda: pl.BlockSpec((None, nb, MEM_LEN, MEM_HEADS, MEM_HEAD_DIM), lambda i: (0, i, 0, 0, 0))
    shape3 = (M_TAIL, MEM_HEADS, MEM_HEAD_DIM)
    out = pl.pallas_call(
        _xattn_tail_kernel,
        grid=(DEC_BATCH // nb,),
        in_specs=[row(), row(), kv(), kv()],
        out_specs=row(),
        out_shape=jax.ShapeDtypeStruct(shape3, F32),
        compiler_params=_params(("arbitrary",)),
        name="xattn_tail",
    )(cq.reshape(shape3), scg.reshape(shape3), ck, cv)
    return out.reshape(M_TAIL, MEM_WIDTH)


def kernel(x_prompt, x_sample, state_ret, cache_mem_k, cache_mem_v, mem_prompt, g_pre, w_in, g_anorm,
           w_s, b_s, g_ret, g_mem, w_mem_kv, w_out_a, w_out_b, w_out_c, w_out, g_final):
    assert DEPTH == 1 and w_in.shape == (DEPTH, D_MODEL, IN_WIDTH)
    xp = x_prompt.reshape(M_MAIN, D_MODEL)
    xs = x_sample.reshape(M_TAIL, D_MODEL)
    win = w_in.reshape(D_MODEL, IN_WIDTH)

    h_m = _rmsnorm(xp, g_pre[0], 512, BF16)
    h_t = _rmsnorm(xs, g_pre[0], M_TAIL, BF16)
    cos_m, sin_m, cos_t, sin_t = _rope_tables()

    def inproj(offs, n_cols, epilogue, out_dtypes, name, tn=1024, tm=1024, exts=(), **kw):
        return _wres_matmul([(h_m, h_t, win, o) for o in offs], list(exts), [out_dtypes],
                            n_cols=n_cols, tn=tn, tm=tm, sub=256, epilogue=epilogue, name=name, **kw)

    rope = [(cos_m, cos_t, "rope", 0), (sin_m, sin_t, "rope", 0)]
    ug_m, ug_t = inproj([OFF_AU, OFF_AG], A_WIDTH, _epi_gelu_silu, (BF16, F32), "inproj_ug", tn=512)
    gv_m, gv_t = inproj([OFF_AV], A_WIDTH, _epi_gelu, (BF16, F32), "inproj_gv")
    q_m, q_t = inproj([OFF_RQ], RET_WIDTH, functools.partial(_epi_rope, 1.0), (BF16, F32),
                      "inproj_q", exts=rope)
    k_m, k_t = inproj([OFF_RK], RET_WIDTH, functools.partial(_epi_rope, RET_HEAD_DIM ** -0.5),
                      (BF16, F32), "inproj_k", exts=rope)
    v_m, v_t = inproj([OFF_RV], RET_WIDTH, _epi_identity, (BF16, F32), "inproj_v")
    sg_m, sg_t = inproj([OFF_RG], RET_WIDTH, _epi_silu, (BF16, F32), "inproj_rg")
    cq_m, cq_t = inproj([OFF_CQ], MEM_WIDTH, _epi_identity, (BF16, F32), "inproj_cq")
    scg_m, scg_t = inproj([OFF_CG], MEM_WIDTH, _epi_silu, (BF16, F32), "inproj_cg")
    gates_tm = 1024
    gs_m, gs_t, b_t, ret_s = inproj(
        [OFF_GATES], N_BRANCH * D_MODEL, _epi_sigmoid, (BF16, BF16), "inproj_gates", tm=gates_tm,
        side=_ret_tail_side(q_t, k_t, v_t, sg_t, g_ret[0], state_ret, M_MAIN // gates_tm),
        vmem_bytes=63 * MIB)

    a_m = _spatial_main(ug_m, gv_m, g_anorm[0], w_s[0], b_s[0])
    a_t, vn_t = _spatial_tail(ug_t, gv_t, g_anorm[0], w_s[0], b_s[0])

    b_m, ret_p = _ret_main(q_m, k_m, v_m, sg_m, g_ret[0])

    hm = _rmsnorm(mem_prompt.reshape(BATCH * MEM_LEN, D_MODEL), g_mem[0], 512, BF16)
    wkv = w_mem_kv.reshape(D_MODEL, 2 * MEM_WIDTH)
    (mk,) = _wres_matmul([(hm, None, wkv, 0)], [], [(F32, None)], n_cols=MEM_WIDTH, tn=512, tm=512,
                         sub=256, epilogue=_epi_identity, name="mem_k")
    (mv,) = _wres_matmul([(hm, None, wkv, MEM_WIDTH)], [], [(F32, None)], n_cols=MEM_WIDTH, tn=512,
                         tm=512, sub=256, epilogue=_epi_identity, name="mem_v")
    c_m = _xattn_main(cq_m, mk, mv, scg_m)
    c_t = _xattn_tail(cq_t, scg_t, cache_mem_k, cache_mem_v)

    merged_m, merged_t = _wres_matmul(
        [(a_m, a_t, w_out_a.reshape(A_WIDTH, D_MODEL), 0),
         (b_m, b_t, w_out_b.reshape(RET_WIDTH, D_MODEL), 0),
         (c_m, c_t, w_out_c.reshape(MEM_WIDTH, D_MODEL), 0)],
        [(gs_m, gs_t, "tile", b * D_MODEL) for b in range(N_BRANCH)],
        [(BF16, BF16)], n_cols=D_MODEL, tn=512, tm=1024, sub=256, epilogue=_epi_merge, name="merge")
    y_m, y_t = _wres_matmul(
        [(merged_m, merged_t, w_out.reshape(D_MODEL, D_MODEL), 0)],
        [(xp, xs, "tile", 0)], [(F32, F32)], n_cols=D_MODEL, tn=512, tm=1024, sub=256,
        epilogue=_epi_residual, name="outproj")
    y_prompt = _rmsnorm(y_m, g_final, 512, F32).reshape(BATCH, SEQ, D_MODEL)
    y_sample = _rmsnorm(y_t, g_final, M_TAIL, F32).reshape(DEC_BATCH, DEC_SEQ, D_MODEL)

    return (y_prompt, y_sample, ret_p,
            mk.reshape(DEPTH, BATCH, MEM_LEN, MEM_HEADS, MEM_HEAD_DIM),
            mv.reshape(DEPTH, BATCH, MEM_LEN, MEM_HEADS, MEM_HEAD_DIM),
            ret_s,
            vn_t.reshape(DEPTH, DEC_BATCH, DEC_SEQ, A_WIDTH))
```

```python
import functools

import jax
import jax.numpy as jnp
from jax import lax
from jax.experimental import pallas as pl
from jax.experimental.pallas import tpu as pltpu

D_MODEL = 4096
BATCH = 4
SEQ = 2048
DEPTH = 1
DEC_BATCH = 128
DEC_SEQ = 1
PAST_LEN = 16384

CHUNK = 128
A_WIDTH = 2048
A_GROUPS = 4
RET_HEADS = 8
RET_HEAD_DIM = 256
RET_WIDTH = RET_HEADS * RET_HEAD_DIM
MEM_LEN = 256
MEM_HEADS = 4
MEM_HEAD_DIM = 256
MEM_WIDTH = MEM_HEADS * MEM_HEAD_DIM
N_BRANCH = 3
ROPE_BASE = 10000.0
EPS = 1e-6

OFF_AU = 0
OFF_AV = OFF_AU + A_WIDTH
OFF_AG = OFF_AV + A_WIDTH
OFF_RQ = OFF_AG + A_WIDTH
OFF_RK = OFF_RQ + RET_WIDTH
OFF_RV = OFF_RK + RET_WIDTH
OFF_RG = OFF_RV + RET_WIDTH
OFF_CQ = OFF_RG + RET_WIDTH
OFF_CG = OFF_CQ + MEM_WIDTH
OFF_GATES = OFF_CG + MEM_WIDTH
IN_WIDTH = OFF_GATES + N_BRANCH * D_MODEL

M_MAIN = BATCH * SEQ
M_TAIL = DEC_BATCH * DEC_SEQ
ROPE_HALF = RET_HEAD_DIM // 2

F32 = jnp.float32
BF16 = jnp.bfloat16

MIB = 1024 * 1024
VMEM_BUDGET_BYTES = 56 * MIB


def _params(semantics, vmem_bytes=VMEM_BUDGET_BYTES):
    return pltpu.CompilerParams(dimension_semantics=semantics, vmem_limit_bytes=vmem_bytes)


def _rmsnorm_kernel(x_ref, g_ref, o_ref):
    x = x_ref[...].astype(F32)
    y = x * lax.rsqrt(jnp.mean(x * x, axis=-1, keepdims=True) + EPS)
    o_ref[...] = (y * g_ref[...].astype(F32)).astype(o_ref.dtype)


def _rmsnorm(x, g, rows, out_dtype):
    m, d = x.shape
    return pl.pallas_call(
        _rmsnorm_kernel,
        grid=(m // rows,),
        in_specs=[pl.BlockSpec((rows, d), lambda i: (i, 0)),
                  pl.BlockSpec((1, d), lambda i: (0, 0))],
        out_specs=pl.BlockSpec((rows, d), lambda i: (i, 0)),
        out_shape=jax.ShapeDtypeStruct((m, d), out_dtype),
        compiler_params=_params(("arbitrary",)),
        name="rmsnorm",
    )(x, g.reshape(1, d))


def _rope_kernel(cm_ref, sm_ref, ct_ref, st_ref):
    def table(rows, pos):
        j = lax.broadcasted_iota(jnp.int32, (rows, ROPE_HALF), 1).astype(F32)
        inv = ROPE_BASE ** (-j / ROPE_HALF)
        return pos.astype(F32) * inv

    ang = table(SEQ, lax.broadcasted_iota(jnp.int32, (SEQ, ROPE_HALF), 0))
    cm_ref[...] = jnp.cos(ang)
    sm_ref[...] = jnp.sin(ang)
    r = lax.broadcasted_iota(jnp.int32, (M_TAIL, ROPE_HALF), 0)
    t = jnp.zeros_like(r) if DEC_SEQ == 1 else lax.rem(r, DEC_SEQ)
    ang_t = table(M_TAIL, PAST_LEN + t)
    ct_ref[...] = jnp.cos(ang_t)
    st_ref[...] = jnp.sin(ang_t)


def _rope_tables():
    return pl.pallas_call(
        _rope_kernel,
        out_shape=(jax.ShapeDtypeStruct((SEQ, ROPE_HALF), F32),
                   jax.ShapeDtypeStruct((SEQ, ROPE_HALF), F32),
                   jax.ShapeDtypeStruct((M_TAIL, ROPE_HALF), F32),
                   jax.ShapeDtypeStruct((M_TAIL, ROPE_HALF), F32)),
        name="rope_tables",
    )()


CAST_ROWS = 256


def _wres_kernel(*refs, lhs_of, w_offs, n_tiles, n_ext, n_out, mt, has_tail, sub, epilogue,
                 side_fn, n_side_in, n_side_out, main_transposed):
    refs = list(refs)
    n_pairs, n_lhs = len(lhs_of), max(lhs_of) + 1

    def take(k):
        out = refs[:k]
        del refs[:k]
        return out

    lhs_main = take(n_lhs)
    lhs_tail = take(n_lhs) if has_tail else []
    w_hbm = take(n_pairs)
    ext_main = take(n_ext)
    ext_tail = take(n_ext) if has_tail else []
    side_in = take(n_side_in)
    out_main = take(n_out)
    out_tail = take(n_out) if has_tail else []
    side_out = take(n_side_out)
    wf = take(n_pairs)
    wb = take(n_pairs)
    (sem,) = take(1)

    n = pl.program_id(0)
    m = pl.program_id(1)
    tn = wb[0].shape[1]

    def w_copy(p, tile):
        col = pl.multiple_of(w_offs[p] + tile * tn, tn)
        return pltpu.make_async_copy(w_hbm[p].at[:, pl.ds(col, tn)], wf[p], sem.at[p])

    @pl.when(m == 0)
    def _weights():
        @pl.when(n == 0)
        def _first():
            for p in range(n_pairs):
                w_copy(p, 0).start()

        for p in range(n_pairs):
            w_copy(p, n).wait()

            def body(i, carry, wf_ref=wf[p], wb_ref=wb[p]):
                r = pl.multiple_of(i * CAST_ROWS, CAST_ROWS)
                wb_ref[pl.ds(r, CAST_ROWS), :] = wf_ref[pl.ds(r, CAST_ROWS), :].astype(BF16)
                return carry
            lax.fori_loop(0, wf[p].shape[0] // CAST_ROWS, body, 0)

        @pl.when(n + 1 < n_tiles)
        def _next():
            for p in range(n_pairs):
                w_copy(p, n + 1).start()

    def body(lhs, ext, outs, transposed=False):
        for c in range(0, tn, sub):
            accs = [jnp.dot(lhs[i][...].astype(BF16), b[:, c:c + sub], preferred_element_type=F32)
                    for i, b in zip(lhs_of, wb)]
            res = epilogue(accs, ext, c, sub, n)
            for o, r in zip(outs, res):
                if transposed:
                    o[c:c + sub, :] = r.T.astype(o.dtype)
                else:
                    o[:, c:c + sub] = r.astype(o.dtype)

    body(lhs_main, ext_main, out_main, main_transposed)
    if side_fn is not None:
        side_fn(n, m, *side_in, *side_out)
    if has_tail:
        @pl.when(m == mt - 1)
        def _tail():
            body(lhs_tail, ext_tail, out_tail)


def _wres_matmul(pairs, exts, outs, *, n_cols, tn, tm, sub, epilogue, name, side=None,
                 vmem_bytes=VMEM_BUDGET_BYTES, main_transposed=False):
    m_main = pairs[0][0].shape[0]
    has_tail = pairs[0][1] is not None
    mt = m_main // tm
    n_tiles = n_cols // tn
    grid = (n_tiles, mt)

    def mrow(m):
        return m

    lhs, lhs_of = [], []
    for lm, lt, _, _ in pairs:
        ids = [i for i, (a, _) in enumerate(lhs) if a is lm]
        if not ids:
            lhs.append((lm, lt))
        lhs_of.append(ids[0] if ids else len(lhs) - 1)

    in_specs, args = [], []
    for lm, _ in lhs:
        in_specs.append(pl.BlockSpec((tm, lm.shape[1]), lambda n, m: (mrow(m), 0)))
        args.append(lm)
    if has_tail:
        for _, lt in lhs:
            in_specs.append(pl.BlockSpec(lt.shape, lambda n, m: (0, 0)))
            args.append(lt)
    for _, _, w, off in pairs:
        assert off % tn == 0 and off + n_cols <= w.shape[1]
        in_specs.append(pl.BlockSpec(memory_space=pl.ANY))
        args.append(w)

    def ext_spec(arr, kind, off, tail):
        rows = arr.shape[0] if tail else tm
        if kind == "tile":
            assert off % tn == 0
            if tail:
                return pl.BlockSpec((rows, tn), lambda n, m, o=off // tn: (0, o + n))
            return pl.BlockSpec((rows, tn), lambda n, m, o=off // tn: (mrow(m), o + n))
        assert kind == "rope"
        if tail:
            return pl.BlockSpec(arr.shape, lambda n, m: (0, 0))
        per = arr.shape[0] // tm
        return pl.BlockSpec((tm, arr.shape[1]), lambda n, m: (lax.rem(mrow(m), per), 0))

    for em, _, kind, off in exts:
        in_specs.append(ext_spec(em, kind, off, False))
        args.append(em)
    if has_tail:
        for _, et, kind, off in exts:
            in_specs.append(ext_spec(et, kind, off, True))
            args.append(et)

    out_specs, out_shape = [], []
    for dm, _ in outs:
        if main_transposed:
            out_specs.append(pl.BlockSpec((tn, tm), lambda n, m: (n, mrow(m))))
            out_shape.append(jax.ShapeDtypeStruct((n_cols, m_main), dm))
        else:
            out_specs.append(pl.BlockSpec((tm, tn), lambda n, m: (mrow(m), n)))
            out_shape.append(jax.ShapeDtypeStruct((m_main, n_cols), dm))
    if has_tail:
        m_tail = pairs[0][1].shape[0]
        for _, dt in outs:
            out_specs.append(pl.BlockSpec((m_tail, tn), lambda n, m: (0, n)))
            out_shape.append(jax.ShapeDtypeStruct((m_tail, n_cols), dt))

    if side is not None:
        assert n_tiles * mt >= side["min_steps"]
        in_specs += side["in_specs"]
        args += side["args"]
        out_specs += side["out_specs"]
        out_shape += side["out_shape"]

    scratch = ([pltpu.VMEM((w.shape[0], tn), F32) for _, _, w, _ in pairs]
               + [pltpu.VMEM((w.shape[0], tn), BF16) for _, _, w, _ in pairs]
               + [pltpu.SemaphoreType.DMA((len(pairs),))])
    kern = functools.partial(_wres_kernel, lhs_of=tuple(lhs_of),
                             w_offs=tuple(off for _, _, _, off in pairs), n_tiles=n_tiles,
                             n_ext=len(exts), n_out=len(outs),
                             mt=mt, has_tail=has_tail, sub=sub, epilogue=epilogue,
                             side_fn=side["fn"] if side else None,
                             n_side_in=len(side["args"]) if side else 0,
                             n_side_out=len(side["out_shape"]) if side else 0,
                             main_transposed=main_transposed)
    return pl.pallas_call(
        kern, grid=grid, in_specs=in_specs, out_specs=out_specs, out_shape=out_shape,
        scratch_shapes=scratch, compiler_params=_params(("arbitrary", "arbitrary"), vmem_bytes),
        name=name,
    )(*args)


def _epi_identity(accs, ext, c, sub, n):
    return [accs[0]]


def _epi_gelu(accs, ext, c, sub, n):
    return [jax.nn.gelu(accs[0])]


def _epi_silu(accs, ext, c, sub, n):
    return [jax.nn.silu(accs[0])]


def _epi_sigmoid(accs, ext, c, sub, n):
    return [jax.nn.sigmoid(accs[0])]


def _epi_gelu_silu(accs, ext, c, sub, n):
    return [jax.nn.gelu(accs[0]) * jax.nn.silu(accs[1])]


def _epi_rope(scale, accs, ext, c, sub, n):
    assert sub == RET_HEAD_DIM
    x = accs[0]
    x1, x2 = x[:, :ROPE_HALF], x[:, ROPE_HALF:]
    cos, sin = ext[0][...], ext[1][...]
    out = jnp.concatenate([x1 * cos - x2 * sin, x1 * sin + x2 * cos], axis=-1)
    return [out if scale == 1.0 else out * scale]


def _epi_merge(accs, ext, c, sub, n):
    g0, g1, g2 = (e[:, c:c + sub] for e in ext)
    return [g0 * accs[0] + g1 * accs[1] + g2 * accs[2]]


def _epi_residual(accs, ext, c, sub, n):
    return [ext[0][:, c:c + sub] + accs[0]]


def _layer_norm(x, g):
    mu = jnp.mean(x, axis=-1, keepdims=True)
    var = jnp.mean(jnp.square(x - mu), axis=-1, keepdims=True)
    return (x - mu) * lax.rsqrt(var + EPS) * g


def _spatial_main_kernel(ug_ref, gv_ref, ga_ref, ws_ref, bst_ref, o_ref):
    rows = gv_ref.shape[0]
    gw = A_WIDTH // A_GROUPS
    vn = _layer_norm(gv_ref[...].astype(F32), ga_ref[...]).astype(BF16)
    ri = lax.broadcasted_iota(jnp.int32, (CHUNK, CHUNK), 0)
    ci = lax.broadcasted_iota(jnp.int32, (CHUNK, CHUNK), 1)
    for g in range(A_GROUPS):
        w = jnp.where(ri >= ci, ws_ref[g], 0.0).astype(BF16)
        bias = bst_ref[:, g:g + 1]
        for c in range(rows // CHUNK):
            rs = slice(c * CHUNK, (c + 1) * CHUNK)
            cs = slice(g * gw, (g + 1) * gw)
            sp = jnp.dot(w, vn[rs, cs], preferred_element_type=F32) + bias
            o_ref[rs, cs] = (ug_ref[rs, cs] * sp).astype(o_ref.dtype)


def _spatial_main(ug, gv, g_anorm, w_s, b_s, rows=512):
    m = ug.shape[0]
    return pl.pallas_call(
        _spatial_main_kernel,
        grid=(m // rows,),
        in_specs=[pl.BlockSpec((rows, A_WIDTH), lambda i: (i, 0)),
                  pl.BlockSpec((rows, A_WIDTH), lambda i: (i, 0)),
                  pl.BlockSpec((1, A_WIDTH), lambda i: (0, 0)),
                  pl.BlockSpec((A_GROUPS, CHUNK, CHUNK), lambda i: (0, 0, 0)),
                  pl.BlockSpec((CHUNK, A_GROUPS), lambda i: (0, 0))],
        out_specs=pl.BlockSpec((rows, A_WIDTH), lambda i: (i, 0)),
        out_shape=jax.ShapeDtypeStruct((m, A_WIDTH), BF16),
        compiler_params=_params(("arbitrary",)),
        name="spatial_main",
    )(ug, gv, g_anorm.reshape(1, A_WIDTH), w_s, b_s.T)


def _spatial_tail_kernel(ug_ref, gv_ref, ga_ref, ws_ref, bs_ref, o_ref, vn_ref):
    gw = A_WIDTH // A_GROUPS
    vn = _layer_norm(gv_ref[...], ga_ref[...])
    vn_ref[...] = vn
    for g in range(A_GROUPS):
        cs = slice(g * gw, (g + 1) * gw)
        sp = vn[:, cs] * ws_ref[g, 0:1, 0:1] + bs_ref[g:g + 1, 0:1]
        o_ref[:, cs] = (ug_ref[:, cs] * sp).astype(o_ref.dtype)


def _spatial_tail(ug, gv, g_anorm, w_s, b_s):
    m = ug.shape[0]
    return pl.pallas_call(
        _spatial_tail_kernel,
        out_shape=(jax.ShapeDtypeStruct((m, A_WIDTH), BF16),
                   jax.ShapeDtypeStruct((m, A_WIDTH), F32)),
        compiler_params=_params(None),
        name="spatial_tail",
    )(ug, gv, g_anorm.reshape(1, A_WIDTH), w_s, b_s)


def _log_gamma(shape, head):
    return jnp.log1p(-jnp.exp2(jnp.full(shape, -5.0, F32) - head.astype(F32)))


def _group_norm(o, g):
    mu = jnp.mean(o, axis=-1, keepdims=True)
    var = jnp.mean(jnp.square(o - mu), axis=-1, keepdims=True)
    return (o - mu) * lax.rsqrt(var + EPS) * g


def _ret_main_kernel(*refs, side_fn, n_side_in):
    q_ref, kt_ref, v_ref, sg_ref, gr_ref = refs[:5]
    side_in = refs[5:5 + n_side_in]
    o_ref, r_ref = refs[5 + n_side_in:7 + n_side_in]
    side_out = refs[7 + n_side_in:]
    head = pl.program_id(1)
    L, dk = CHUNK, RET_HEAD_DIM
    ri = lax.broadcasted_iota(jnp.int32, (L, L), 0).astype(F32)
    ci = lax.broadcasted_iota(jnp.int32, (L, L), 1).astype(F32)
    diff = ri - ci
    decay_in = jnp.where(diff >= 0, jnp.exp(_log_gamma((L, L), head) * jnp.maximum(diff, 0.0)), 0.0)
    rw = lax.broadcasted_iota(jnp.int32, (L, dk), 0).astype(F32)
    decay_q = jnp.exp(_log_gamma((L, dk), head) * (rw + 1.0))
    cw = lax.broadcasted_iota(jnp.int32, (dk, L), 1).astype(F32)
    decay_kt = jnp.exp(_log_gamma((dk, L), head) * (L - 1.0 - cw))
    decay_blk = jnp.exp(_log_gamma((dk, dk), head) * L)
    gr = gr_ref[...]

    R = jnp.zeros((dk, dk), F32)
    for i in range(SEQ // L):
        rs = slice(i * L, (i + 1) * L)
        qi, kti, vi = q_ref[rs, :], kt_ref[:, rs], v_ref[rs, :]
        s = jnp.dot(qi, kti, preferred_element_type=F32) * decay_in
        o = (jnp.dot(s.astype(BF16), vi, preferred_element_type=F32)
             + jnp.dot(qi, R.astype(BF16), preferred_element_type=F32) * decay_q)
        kdt = (kti.astype(F32) * decay_kt).astype(BF16)
        R = R * decay_blk + jnp.dot(kdt, vi, preferred_element_type=F32)
        o_ref[rs, :] = (_group_norm(o, gr) * sg_ref[rs, :]).astype(o_ref.dtype)
    r_ref[...] = R
    if side_fn is not None:
        side_fn(*side_in, *side_out)


def _ret_main(q, kt, v, sg, g_ret, side=None):
    dk = RET_HEAD_DIM
    blk = lambda: pl.BlockSpec((SEQ, dk), lambda b, h: (b, h))
    in_specs = [blk(), pl.BlockSpec((dk, SEQ), lambda b, h: (h, b)), blk(), blk(),
                pl.BlockSpec((1, dk), lambda b, h: (0, h))]
    out_specs = [blk(), pl.BlockSpec((None, None, None, dk, dk), lambda b, h: (0, b, h, 0, 0))]
    out_shape = [jax.ShapeDtypeStruct((M_MAIN, RET_WIDTH), BF16),
                 jax.ShapeDtypeStruct((DEPTH, BATCH, RET_HEADS, dk, dk), F32)]
    args = [q, kt, v, sg, g_ret.reshape(1, RET_WIDTH)]
    if side is not None:
        assert BATCH * RET_HEADS >= side["min_steps"]
        in_specs += side["in_specs"]
        args += side["args"]
        out_specs += side["out_specs"]
        out_shape += side["out_shape"]
    kern = functools.partial(_ret_main_kernel, side_fn=side["fn"] if side else None,
                             n_side_in=len(side["args"]) if side else 0)
    return pl.pallas_call(
        kern, grid=(BATCH, RET_HEADS), in_specs=in_specs, out_specs=out_specs, out_shape=out_shape,
        compiler_params=_params(("arbitrary", "arbitrary")), name="ret_main",
    )(*args)


RET_TAIL_ROWS = 16


def _ret_tail_block(head, q_ref, k_ref, v_ref, sg_ref, gr_ref, s_ref, o_ref, so_ref):
    nb, dk = q_ref.shape
    gam_row = jnp.exp(_log_gamma((nb, dk), head))
    gam_st = jnp.exp(_log_gamma((dk, dk), head))
    q, k, v = q_ref[...], k_ref[...], v_ref[...]
    qt, kt = q.T, k.T
    rows = []
    for r in range(nb):
        R = s_ref[r]
        so_ref[r] = R * gam_st + kt[:, r:r + 1] * v[r:r + 1, :]
        rows.append(jnp.sum(qt[:, r:r + 1] * R, axis=0, keepdims=True))
    qr = jnp.concatenate(rows, axis=0)
    qk = jnp.sum(q * k, axis=-1, keepdims=True)
    o = qk * v + qr * gam_row
    o_ref[...] = _group_norm(o, gr_ref[...]) * sg_ref[...]


def _ret_tail_side(q, k, v, sg, g_ret, state, steps_per_tile):
    dk, nb = RET_HEAD_DIM, RET_TAIL_ROWS
    n_blocks = (DEC_BATCH // nb) * RET_HEADS

    def block(n, m):
        return jnp.minimum(n * steps_per_tile + m, n_blocks - 1)

    row = lambda: pl.BlockSpec((nb, dk), lambda n, m: (block(n, m) // RET_HEADS, block(n, m) % RET_HEADS))
    st = lambda: pl.BlockSpec((None, nb, None, dk, dk),
                              lambda n, m: (0, block(n, m) // RET_HEADS, block(n, m) % RET_HEADS, 0, 0))

    def fn(n, m, *refs):
        _ret_tail_block(block(n, m) % RET_HEADS, *refs)

    return dict(
        fn=fn, min_steps=n_blocks,
        args=[q, k, v, sg, g_ret.reshape(1, RET_WIDTH), state],
        in_specs=[row(), row(), row(), row(),
                  pl.BlockSpec((1, dk), lambda n, m: (0, block(n, m) % RET_HEADS)), st()],
        out_specs=[row(), st()],
        out_shape=[jax.ShapeDtypeStruct((M_TAIL, RET_WIDTH), F32),
                   jax.ShapeDtypeStruct(state.shape, F32)])


XATTN_ROWS = 512


def _xattn_main_kernel(q_ref, k_ref, v_ref, g_ref, o_ref):
    k = k_ref[...].astype(BF16)
    v = v_ref[...].astype(BF16)
    for t in range(q_ref.shape[0] // XATTN_ROWS):
        rs = slice(t * XATTN_ROWS, (t + 1) * XATTN_ROWS)
        sc = lax.dot_general(q_ref[rs, :], k, (((1,), (1,)), ((), ())), preferred_element_type=F32)
        sc = sc * (MEM_HEAD_DIM ** -0.5)
        e = jnp.exp(sc - jnp.max(sc, axis=-1, keepdims=True))
        p = e / jnp.sum(e, axis=-1, keepdims=True)
        om = jnp.dot(p.astype(BF16), v, preferred_element_type=F32)
        o_ref[rs, :] = (om * g_ref[rs, :]).astype(o_ref.dtype)


def _xattn_main(cq, mk, mv, scg):
    dh = MEM_HEAD_DIM
    qspec = lambda: pl.BlockSpec((SEQ, dh), lambda b, h: (b, h))
    kvspec = lambda: pl.BlockSpec((MEM_LEN, dh), lambda b, h: (b, h))
    return pl.pallas_call(
        _xattn_main_kernel,
        grid=(BATCH, MEM_HEADS),
        in_specs=[qspec(), kvspec(), kvspec(), qspec()],
        out_specs=qspec(),
        out_shape=jax.ShapeDtypeStruct((M_MAIN, MEM_WIDTH), BF16),
        compiler_params=_params(("arbitrary", "arbitrary")),
        name="xattn_main",
    )(cq, mk, mv, scg)


XATTN_TAIL_ROWS = 4


def _xattn_tail_kernel(q_ref, g_ref, k_ref, v_ref, o_ref):
    for r in range(q_ref.shape[0]):
        q = q_ref[r]
        sc = jnp.sum(k_ref[r] * q[None], axis=-1, keepdims=True) * (MEM_HEAD_DIM ** -0.5)
        e = jnp.exp(sc - jnp.max(sc, axis=0, keepdims=True))
        o_ref[r] = jnp.sum(e * v_ref[r], axis=0) / jnp.sum(e, axis=0) * g_ref[r]


def _xattn_tail_side(cq, scg, ck, cv, steps_per_row):
    nb = XATTN_TAIL_ROWS
    n_blocks = DEC_BATCH // nb

    def block(i, j):
        return jnp.minimum(i * steps_per_row + j, n_blocks - 1)

    row = lambda: pl.BlockSpec((nb, MEM_HEADS, MEM_HEAD_DIM), lambda i, j: (block(i, j), 0, 0))
    kv = lambda: pl.BlockSpec((None, nb, MEM_LEN, MEM_HEADS, MEM_HEAD_DIM),
                              lambda i, j: (0, block(i, j), 0, 0, 0))
    shape3 = (M_TAIL, MEM_HEADS, MEM_HEAD_DIM)
    return dict(fn=_xattn_tail_kernel, min_steps=n_blocks,
                args=[cq.reshape(shape3), scg.reshape(shape3), ck, cv],
                in_specs=[row(), row(), kv(), kv()], out_specs=[row()],
                out_shape=[jax.ShapeDtypeStruct(shape3, F32)])


def kernel(x_prompt, x_sample, state_ret, cache_mem_k, cache_mem_v, mem_prompt, g_pre, w_in, g_anorm,
           w_s, b_s, g_ret, g_mem, w_mem_kv, w_out_a, w_out_b, w_out_c, w_out, g_final):
    assert DEPTH == 1 and w_in.shape == (DEPTH, D_MODEL, IN_WIDTH)
    xp = x_prompt.reshape(M_MAIN, D_MODEL)
    xs = x_sample.reshape(M_TAIL, D_MODEL)
    win = w_in.reshape(D_MODEL, IN_WIDTH)

    h_m = _rmsnorm(xp, g_pre[0], 512, BF16)
    h_t = _rmsnorm(xs, g_pre[0], M_TAIL, BF16)
    cos_m, sin_m, cos_t, sin_t = _rope_tables()

    def inproj(offs, n_cols, epilogue, out_dtypes, name, tn=1024, tm=1024, exts=(), **kw):
        return _wres_matmul([(h_m, h_t, win, o) for o in offs], list(exts), [out_dtypes],
                            n_cols=n_cols, tn=tn, tm=tm, sub=256, epilogue=epilogue, name=name, **kw)

    rope = [(cos_m, cos_t, "rope", 0), (sin_m, sin_t, "rope", 0)]
    ug_m, ug_t = inproj([OFF_AU, OFF_AG], A_WIDTH, _epi_gelu_silu, (BF16, F32), "inproj_ug", tn=512)
    gv_m, gv_t = inproj([OFF_AV], A_WIDTH, _epi_gelu, (BF16, F32), "inproj_gv")
    q_m, q_t = inproj([OFF_RQ], RET_WIDTH, functools.partial(_epi_rope, 1.0), (BF16, F32),
                      "inproj_q", exts=rope)
    kt_m, k_t = inproj([OFF_RK], RET_WIDTH, functools.partial(_epi_rope, RET_HEAD_DIM ** -0.5),
                       (BF16, F32), "inproj_k", exts=rope, main_transposed=True)
    v_m, v_t = inproj([OFF_RV], RET_WIDTH, _epi_identity, (BF16, F32), "inproj_v")
    sg_m, sg_t = inproj([OFF_RG], RET_WIDTH, _epi_silu, (BF16, F32), "inproj_rg")
    cq_m, cq_t = inproj([OFF_CQ], MEM_WIDTH, _epi_identity, (BF16, F32), "inproj_cq")
    scg_m, scg_t = inproj([OFF_CG], MEM_WIDTH, _epi_silu, (BF16, F32), "inproj_cg")
    gates_tm = 1024
    gs_m, gs_t, b_t, ret_s = inproj(
        [OFF_GATES], N_BRANCH * D_MODEL, _epi_sigmoid, (BF16, BF16), "inproj_gates", tm=gates_tm,
        side=_ret_tail_side(q_t, k_t, v_t, sg_t, g_ret[0], state_ret, M_MAIN // gates_tm),
        vmem_bytes=63 * MIB)

    a_m = _spatial_main(ug_m, gv_m, g_anorm[0], w_s[0], b_s[0])
    a_t, vn_t = _spatial_tail(ug_t, gv_t, g_anorm[0], w_s[0], b_s[0])

    b_m, ret_p, c_t3 = _ret_main(
        q_m, kt_m, v_m, sg_m, g_ret[0],
        side=_xattn_tail_side(cq_t, scg_t, cache_mem_k, cache_mem_v, RET_HEADS))
    c_t = c_t3.reshape(M_TAIL, MEM_WIDTH)

    hm = _rmsnorm(mem_prompt.reshape(BATCH * MEM_LEN, D_MODEL), g_mem[0], 512, BF16)
    wkv = w_mem_kv.reshape(D_MODEL, 2 * MEM_WIDTH)
    (mk,) = _wres_matmul([(hm, None, wkv, 0)], [], [(F32, None)], n_cols=MEM_WIDTH, tn=512, tm=512,
                         sub=256, epilogue=_epi_identity, name="mem_k")
    (mv,) = _wres_matmul([(hm, None, wkv, MEM_WIDTH)], [], [(F32, None)], n_cols=MEM_WIDTH, tn=512,
                         tm=512, sub=256, epilogue=_epi_identity, name="mem_v")
    c_m = _xattn_main(cq_m, mk, mv, scg_m)

    merged_m, merged_t = _wres_matmul(
        [(a_m, a_t, w_out_a.reshape(A_WIDTH, D_MODEL), 0),
         (b_m, b_t, w_out_b.reshape(RET_WIDTH, D_MODEL), 0),
         (c_m, c_t, w_out_c.reshape(MEM_WIDTH, D_MODEL), 0)],
        [(gs_m, gs_t, "tile", b * D_MODEL) for b in range(N_BRANCH)],
        [(BF16, BF16)], n_cols=D_MODEL, tn=512, tm=1024, sub=256, epilogue=_epi_merge, name="merge")
    y_m, y_t = _wres_matmul(
        [(merged_m, merged_t, w_out.reshape(D_MODEL, D_MODEL), 0)],
        [(xp, xs, "tile", 0)], [(F32, F32)], n_cols=D_MODEL, tn=512, tm=1024, sub=256,
        epilogue=_epi_residual, name="outproj")
    y_prompt = _rmsnorm(y_m, g_final, 512, F32).reshape(BATCH, SEQ, D_MODEL)
    y_sample = _rmsnorm(y_t, g_final, M_TAIL, F32).reshape(DEC_BATCH, DEC_SEQ, D_MODEL)

    return (y_prompt, y_sample, ret_p,
            mk.reshape(DEPTH, BATCH, MEM_LEN, MEM_HEADS, MEM_HEAD_DIM),
            mv.reshape(DEPTH, BATCH, MEM_LEN, MEM_HEADS, MEM_HEAD_DIM),
            ret_s,
            vn_t.reshape(DEPTH, DEC_BATCH, DEC_SEQ, A_WIDTH))
```

```python
import functools

import jax
import jax.numpy as jnp
from jax import lax
from jax.experimental import pallas as pl
from jax.experimental.pallas import tpu as pltpu

D_MODEL = 4096
BATCH = 4
SEQ = 2048
DEPTH = 1
DEC_BATCH = 128
DEC_SEQ = 1
PAST_LEN = 16384

CHUNK = 128
A_WIDTH = 2048
A_GROUPS = 4
RET_HEADS = 8
RET_HEAD_DIM = 256
RET_WIDTH = RET_HEADS * RET_HEAD_DIM
MEM_LEN = 256
MEM_HEADS = 4
MEM_HEAD_DIM = 256
MEM_WIDTH = MEM_HEADS * MEM_HEAD_DIM
N_BRANCH = 3
ROPE_BASE = 10000.0
EPS = 1e-6

OFF_AU = 0
OFF_AV = OFF_AU + A_WIDTH
OFF_AG = OFF_AV + A_WIDTH
OFF_RQ = OFF_AG + A_WIDTH
OFF_RK = OFF_RQ + RET_WIDTH
OFF_RV = OFF_RK + RET_WIDTH
OFF_RG = OFF_RV + RET_WIDTH
OFF_CQ = OFF_RG + RET_WIDTH
OFF_CG = OFF_CQ + MEM_WIDTH
OFF_GATES = OFF_CG + MEM_WIDTH
IN_WIDTH = OFF_GATES + N_BRANCH * D_MODEL

M_MAIN = BATCH * SEQ
M_TAIL = DEC_BATCH * DEC_SEQ
ROPE_HALF = RET_HEAD_DIM // 2

F32 = jnp.float32
BF16 = jnp.bfloat16

MIB = 1024 * 1024
VMEM_BUDGET_BYTES = 56 * MIB
VMEM_HOST_BYTES = 63 * MIB


def _params(semantics, vmem_bytes=VMEM_BUDGET_BYTES):
    return pltpu.CompilerParams(dimension_semantics=semantics, vmem_limit_bytes=vmem_bytes)


def _rmsnorm_kernel(x_ref, g_ref, o_ref):
    x = x_ref[...].astype(F32)
    y = x * lax.rsqrt(jnp.mean(x * x, axis=-1, keepdims=True) + EPS)
    o_ref[...] = (y * g_ref[...].astype(F32)).astype(o_ref.dtype)


def _rmsnorm(x, g, rows, out_dtype):
    m, d = x.shape
    return pl.pallas_call(
        _rmsnorm_kernel,
        grid=(m // rows,),
        in_specs=[pl.BlockSpec((rows, d), lambda i: (i, 0)),
                  pl.BlockSpec((1, d), lambda i: (0, 0))],
        out_specs=pl.BlockSpec((rows, d), lambda i: (i, 0)),
        out_shape=jax.ShapeDtypeStruct((m, d), out_dtype),
        compiler_params=_params(("arbitrary",)),
        name="rmsnorm",
    )(x, g.reshape(1, d))


def _rope_kernel(cm_ref, sm_ref, ct_ref, st_ref):
    def table(rows, pos):
        j = lax.broadcasted_iota(jnp.int32, (rows, ROPE_HALF), 1).astype(F32)
        inv = ROPE_BASE ** (-j / ROPE_HALF)
        return pos.astype(F32) * inv

    ang = table(SEQ, lax.broadcasted_iota(jnp.int32, (SEQ, ROPE_HALF), 0))
    cm_ref[...] = jnp.cos(ang)
    sm_ref[...] = jnp.sin(ang)
    r = lax.broadcasted_iota(jnp.int32, (M_TAIL, ROPE_HALF), 0)
    t = jnp.zeros_like(r) if DEC_SEQ == 1 else lax.rem(r, DEC_SEQ)
    ang_t = table(M_TAIL, PAST_LEN + t)
    ct_ref[...] = jnp.cos(ang_t)
    st_ref[...] = jnp.sin(ang_t)


def _rope_tables():
    return pl.pallas_call(
        _rope_kernel,
        out_shape=(jax.ShapeDtypeStruct((SEQ, ROPE_HALF), F32),
                   jax.ShapeDtypeStruct((SEQ, ROPE_HALF), F32),
                   jax.ShapeDtypeStruct((M_TAIL, ROPE_HALF), F32),
                   jax.ShapeDtypeStruct((M_TAIL, ROPE_HALF), F32)),
        name="rope_tables",
    )()


CAST_ROWS = 256


def _wres_kernel(*refs, lhs_of, w_cols, n_tiles, n_ext, n_out, mt, has_tail, sub, epilogue,
                 side_fn, n_side_in, n_side_out, main_transposed):
    refs = list(refs)
    n_pairs, n_lhs = len(lhs_of), max(lhs_of) + 1

    def take(k):
        out = refs[:k]
        del refs[:k]
        return out

    lhs_main = take(n_lhs)
    lhs_tail = take(n_lhs) if has_tail else []
    w_hbm = take(n_pairs)
    ext_main = take(n_ext)
    ext_tail = take(n_ext) if has_tail else []
    side_in = take(n_side_in)
    out_main = take(n_out)
    out_tail = take(n_out) if has_tail else []
    side_out = take(n_side_out)
    wf = take(n_pairs)
    wb = take(n_pairs)
    (sem,) = take(1)

    n = pl.program_id(0)
    m = pl.program_id(1)
    tn = wb[0].shape[1]

    def w_copy(p, tile):
        col = w_cols[p][0][1] + tile * tn
        for first_tile, first_col in w_cols[p][1:]:
            col = jnp.where(tile >= first_tile, first_col + (tile - first_tile) * tn, col)
        col = pl.multiple_of(col, tn)
        return pltpu.make_async_copy(w_hbm[p].at[:, pl.ds(col, tn)], wf[p], sem.at[p])

    @pl.when(m == 0)
    def _weights():
        @pl.when(n == 0)
        def _first():
            for p in range(n_pairs):
                w_copy(p, 0).start()

        for p in range(n_pairs):
            w_copy(p, n).wait()

            def body(i, carry, wf_ref=wf[p], wb_ref=wb[p]):
                r = pl.multiple_of(i * CAST_ROWS, CAST_ROWS)
                wb_ref[pl.ds(r, CAST_ROWS), :] = wf_ref[pl.ds(r, CAST_ROWS), :].astype(BF16)
                return carry
            lax.fori_loop(0, wf[p].shape[0] // CAST_ROWS, body, 0)

        @pl.when(n + 1 < n_tiles)
        def _next():
            for p in range(n_pairs):
                w_copy(p, n + 1).start()

    def body(lhs, ext, outs, transposed=False):
        for c in range(0, tn, sub):
            accs = [jnp.dot(lhs[i][...].astype(BF16), b[:, c:c + sub], preferred_element_type=F32)
                    for i, b in zip(lhs_of, wb)]
            res = epilogue(accs, ext, c, sub, n)
            for o, r in zip(outs, res):
                if transposed:
                    o[c:c + sub, :] = r.T.astype(o.dtype)
                else:
                    o[:, c:c + sub] = r.astype(o.dtype)

    body(lhs_main, ext_main, out_main, main_transposed)
    if side_fn is not None:
        side_fn(n, m, *side_in, *side_out)
    if has_tail:
        @pl.when(m == mt - 1)
        def _tail():
            body(lhs_tail, ext_tail, out_tail)


def _wres_matmul(pairs, exts, outs, *, n_cols, tn, tm, sub, epilogue, name, side=None,
                 vmem_bytes=VMEM_BUDGET_BYTES, main_transposed=False):
    m_main = pairs[0][0].shape[0]
    has_tail = pairs[0][1] is not None
    mt = m_main // tm
    n_tiles = n_cols // tn
    grid = (n_tiles, mt)

    lhs, lhs_of = [], []
    for lm, lt, _, _ in pairs:
        ids = [i for i, (a, _) in enumerate(lhs) if a is lm]
        if not ids:
            lhs.append((lm, lt))
        lhs_of.append(ids[0] if ids else len(lhs) - 1)

    in_specs, args = [], []
    for lm, _ in lhs:
        in_specs.append(pl.BlockSpec((tm, lm.shape[1]), lambda n, m: (m, 0)))
        args.append(lm)
    if has_tail:
        for _, lt in lhs:
            in_specs.append(pl.BlockSpec(lt.shape, lambda n, m: (0, 0)))
            args.append(lt)
    w_cols = []
    for _, _, w, cols in pairs:
        segs = ((0, cols),) if isinstance(cols, int) else tuple(cols)
        for i, (first_tile, first_col) in enumerate(segs):
            last_tile = segs[i + 1][0] if i + 1 < len(segs) else n_tiles
            assert first_col % tn == 0 and first_col + (last_tile - first_tile) * tn <= w.shape[1]
        w_cols.append(segs)
        in_specs.append(pl.BlockSpec(memory_space=pl.ANY))
        args.append(w)

    def ext_spec(arr, kind, off, tail):
        rows = arr.shape[0] if tail else tm
        if kind == "tile":
            assert off % tn == 0
            if tail:
                return pl.BlockSpec((rows, tn), lambda n, m, o=off // tn: (0, o + n))
            return pl.BlockSpec((rows, tn), lambda n, m, o=off // tn: (m, o + n))
        assert kind == "rope"
        if tail:
            return pl.BlockSpec(arr.shape, lambda n, m: (0, 0))
        per = arr.shape[0] // tm
        return pl.BlockSpec((tm, arr.shape[1]), lambda n, m: (lax.rem(m, per), 0))

    for em, _, kind, off in exts:
        in_specs.append(ext_spec(em, kind, off, False))
        args.append(em)
    if has_tail:
        for _, et, kind, off in exts:
            in_specs.append(ext_spec(et, kind, off, True))
            args.append(et)

    out_specs, out_shape = [], []
    for dm, _ in outs:
        if main_transposed:
            out_specs.append(pl.BlockSpec((tn, tm), lambda n, m: (n, m)))
            out_shape.append(jax.ShapeDtypeStruct((n_cols, m_main), dm))
        else:
            out_specs.append(pl.BlockSpec((tm, tn), lambda n, m: (m, n)))
            out_shape.append(jax.ShapeDtypeStruct((m_main, n_cols), dm))
    if has_tail:
        m_tail = pairs[0][1].shape[0]
        for _, dt in outs:
            out_specs.append(pl.BlockSpec((m_tail, tn), lambda n, m: (0, n)))
            out_shape.append(jax.ShapeDtypeStruct((m_tail, n_cols), dt))

    if side is not None:
        assert n_tiles * mt >= side["n_blocks"]
        step = lambda n, m: jnp.minimum(n * mt + m, side["n_blocks"] - 1)
        in_specs += [pl.BlockSpec(shape, lambda n, m, f=f: f(step(n, m)))
                     for shape, f in side["in_specs"]]
        args += side["args"]
        out_specs += [pl.BlockSpec(shape, lambda n, m, f=f: f(step(n, m)))
                      for shape, f in side["out_specs"]]
        out_shape += side["out_shape"]
        side_fn = lambda n, m, *refs: side["fn"](step(n, m), *refs)
    else:
        side_fn = None

    scratch = ([pltpu.VMEM((w.shape[0], tn), F32) for _, _, w, _ in pairs]
               + [pltpu.VMEM((w.shape[0], tn), BF16) for _, _, w, _ in pairs]
               + [pltpu.SemaphoreType.DMA((len(pairs),))])
    kern = functools.partial(_wres_kernel, lhs_of=tuple(lhs_of), w_cols=tuple(w_cols),
                             n_tiles=n_tiles, n_ext=len(exts), n_out=len(outs),
                             mt=mt, has_tail=has_tail, sub=sub, epilogue=epilogue,
                             side_fn=side_fn,
                             n_side_in=len(side["args"]) if side else 0,
                             n_side_out=len(side["out_shape"]) if side else 0,
                             main_transposed=main_transposed)
    return pl.pallas_call(
        kern, grid=grid, in_specs=in_specs, out_specs=out_specs, out_shape=out_shape,
        scratch_shapes=scratch, compiler_params=_params(("arbitrary", "arbitrary"), vmem_bytes),
        name=name,
    )(*args)


def _side(fn, n_blocks, args, in_specs, out_specs, out_shape):
    return dict(fn=fn, n_blocks=n_blocks, args=args, in_specs=in_specs, out_specs=out_specs,
                out_shape=out_shape)


def _epi_identity(accs, ext, c, sub, n):
    return [accs[0]]


def _epi_sigmoid(accs, ext, c, sub, n):
    return [jax.nn.sigmoid(accs[0])]


def _epi_gelu_silu(accs, ext, c, sub, n):
    return [jax.nn.gelu(accs[0]) * jax.nn.silu(accs[1])]


def _epi_gelu(accs, ext, c, sub, n):
    return [jax.nn.gelu(accs[0])]


def _epi_silu(accs, ext, c, sub, n):
    return [jax.nn.silu(accs[0])]


def _epi_rope(scale, accs, ext, c, sub, n):
    assert sub == RET_HEAD_DIM
    x = accs[0]
    x1, x2 = x[:, :ROPE_HALF], x[:, ROPE_HALF:]
    cos, sin = ext[0][...], ext[1][...]
    out = jnp.concatenate([x1 * cos - x2 * sin, x1 * sin + x2 * cos], axis=-1)
    return [out if scale == 1.0 else out * scale]


def _epi_merge(accs, ext, c, sub, n):
    g0, g1, g2 = (e[:, c:c + sub] for e in ext)
    return [g0 * accs[0] + g1 * accs[1] + g2 * accs[2]]


def _epi_residual(accs, ext, c, sub, n):
    return [ext[0][:, c:c + sub] + accs[0]]


SPATIAL_ROWS = 256


def _layer_norm(x, g):
    mu = jnp.mean(x, axis=-1, keepdims=True)
    var = jnp.mean(jnp.square(x - mu), axis=-1, keepdims=True)
    return (x - mu) * lax.rsqrt(var + EPS) * g


def _spatial_main_block(ug_ref, gv_ref, ga_ref, ws_ref, bst_ref, o_ref):
    rows = gv_ref.shape[0]
    gw = A_WIDTH // A_GROUPS
    vn = _layer_norm(gv_ref[...].astype(F32), ga_ref[...]).astype(BF16)
    ri = lax.broadcasted_iota(jnp.int32, (CHUNK, CHUNK), 0)
    ci = lax.broadcasted_iota(jnp.int32, (CHUNK, CHUNK), 1)
    for g in range(A_GROUPS):
        w = jnp.where(ri >= ci, ws_ref[g], 0.0).astype(BF16)
        bias = bst_ref[:, g:g + 1]
        for c in range(rows // CHUNK):
            rs = slice(c * CHUNK, (c + 1) * CHUNK)
            cs = slice(g * gw, (g + 1) * gw)
            sp = jnp.dot(w, vn[rs, cs], preferred_element_type=F32) + bias
            o_ref[rs, cs] = (ug_ref[rs, cs] * sp).astype(o_ref.dtype)


def _spatial_main_side(ug, gv_src, g_anorm, w_s, b_s):
    gv, gv_blk = gv_src
    rows = SPATIAL_ROWS
    return _side(
        lambda blk, *refs: _spatial_main_block(*refs), M_MAIN // rows,
        args=[ug, gv, g_anorm.reshape(1, A_WIDTH), w_s, b_s.T],
        in_specs=[((rows, A_WIDTH), lambda i: (i, 0)),
                  ((rows, A_WIDTH), lambda i: (i, gv_blk)),
                  ((1, A_WIDTH), lambda i: (0, 0)),
                  ((A_GROUPS, CHUNK, CHUNK), lambda i: (0, 0, 0)),
                  ((CHUNK, A_GROUPS), lambda i: (0, 0))],
        out_specs=[((rows, A_WIDTH), lambda i: (i, 0))],
        out_shape=[jax.ShapeDtypeStruct((M_MAIN, A_WIDTH), BF16)])


def _spatial_tail_kernel(ug_ref, gv_ref, ga_ref, ws_ref, bs_ref, o_ref, vn_ref, *, gv_col):
    gw = A_WIDTH // A_GROUPS
    vn = _layer_norm(gv_ref[:, gv_col:gv_col + A_WIDTH], ga_ref[...])
    vn_ref[...] = vn
    for g in range(A_GROUPS):
        cs = slice(g * gw, (g + 1) * gw)
        sp = vn[:, cs] * ws_ref[g, 0:1, 0:1] + bs_ref[g:g + 1, 0:1]
        o_ref[:, cs] = (ug_ref[:, cs] * sp).astype(o_ref.dtype)


def _spatial_tail(ug, gv, gv_col, g_anorm, w_s, b_s):
    m = ug.shape[0]
    return pl.pallas_call(
        functools.partial(_spatial_tail_kernel, gv_col=gv_col),
        out_shape=(jax.ShapeDtypeStruct((m, A_WIDTH), BF16),
                   jax.ShapeDtypeStruct((m, A_WIDTH), F32)),
        compiler_params=_params(None),
        name="spatial_tail",
    )(ug, gv, g_anorm.reshape(1, A_WIDTH), w_s, b_s)


def _log_gamma(shape, head):
    return jnp.log1p(-jnp.exp2(jnp.full(shape, -5.0, F32) - head.astype(F32)))


def _group_norm(o, g):
    mu = jnp.mean(o, axis=-1, keepdims=True)
    var = jnp.mean(jnp.square(o - mu), axis=-1, keepdims=True)
    return (o - mu) * lax.rsqrt(var + EPS) * g


def _ret_main_kernel(*refs, side_fn, n_side_in):
    q_ref, kt_ref, v_ref, sg_ref, gr_ref = refs[:5]
    side_in = refs[5:5 + n_side_in]
    o_ref, r_ref = refs[5 + n_side_in:7 + n_side_in]
    side_out = refs[7 + n_side_in:]
    head = pl.program_id(1)
    L, dk = CHUNK, RET_HEAD_DIM
    ri = lax.broadcasted_iota(jnp.int32, (L, L), 0).astype(F32)
    ci = lax.broadcasted_iota(jnp.int32, (L, L), 1).astype(F32)
    diff = ri - ci
    decay_in = jnp.where(diff >= 0, jnp.exp(_log_gamma((L, L), head) * jnp.maximum(diff, 0.0)), 0.0)
    rw = lax.broadcasted_iota(jnp.int32, (L, dk), 0).astype(F32)
    decay_q = jnp.exp(_log_gamma((L, dk), head) * (rw + 1.0))
    cw = lax.broadcasted_iota(jnp.int32, (dk, L), 1).astype(F32)
    decay_kt = jnp.exp(_log_gamma((dk, L), head) * (L - 1.0 - cw))
    decay_blk = jnp.exp(_log_gamma((dk, dk), head) * L)
    gr = gr_ref[...]

    R = jnp.zeros((dk, dk), F32)
    for i in range(SEQ // L):
        rs = slice(i * L, (i + 1) * L)
        qi, kti, vi = q_ref[rs, :], kt_ref[:, rs], v_ref[rs, :]
        s = jnp.dot(qi, kti, preferred_element_type=F32) * decay_in
        o = (jnp.dot(s.astype(BF16), vi, preferred_element_type=F32)
             + jnp.dot(qi, R.astype(BF16), preferred_element_type=F32) * decay_q)
        kdt = (kti.astype(F32) * decay_kt).astype(BF16)
        R = R * decay_blk + jnp.dot(kdt, vi, preferred_element_type=F32)
        o_ref[rs, :] = (_group_norm(o, gr) * sg_ref[rs, :]).astype(o_ref.dtype)
    r_ref[...] = R
    side_fn(pl.program_id(0) * RET_HEADS + head, *side_in, *side_out)


def _ret_main(q, kt, v_src, sg_src, g_ret, side):
    dk = RET_HEAD_DIM
    (v, v_blk), (sg, sg_blk) = v_src, sg_src
    blk = lambda off: pl.BlockSpec((SEQ, dk), lambda b, h: (b, off + h))
    step = lambda b, h: b * RET_HEADS + h
    assert BATCH * RET_HEADS == side["n_blocks"]
    in_specs = [blk(0), pl.BlockSpec((dk, SEQ), lambda b, h: (h, b)), blk(v_blk), blk(sg_blk),
                pl.BlockSpec((1, dk), lambda b, h: (0, h))]
    in_specs += [pl.BlockSpec(shape, lambda b, h, f=f: f(step(b, h))) for shape, f in side["in_specs"]]
    out_specs = [blk(0), pl.BlockSpec((None, None, None, dk, dk), lambda b, h: (0, b, h, 0, 0))]
    out_specs += [pl.BlockSpec(shape, lambda b, h, f=f: f(step(b, h))) for shape, f in side["out_specs"]]
    out_shape = [jax.ShapeDtypeStruct((M_MAIN, RET_WIDTH), BF16),
                 jax.ShapeDtypeStruct((DEPTH, BATCH, RET_HEADS, dk, dk), F32)] + side["out_shape"]
    kern = functools.partial(_ret_main_kernel, side_fn=side["fn"], n_side_in=len(side["args"]))
    return pl.pallas_call(
        kern, grid=(BATCH, RET_HEADS), in_specs=in_specs, out_specs=out_specs, out_shape=out_shape,
        compiler_params=_params(("arbitrary", "arbitrary")), name="ret_main",
    )(q, kt, v, sg, g_ret.reshape(1, RET_WIDTH), *side["args"])


RET_TAIL_ROWS = 16


def _ret_tail_block(blk, q_ref, k_ref, v_ref, sg_ref, gr_ref, s_ref, o_ref, so_ref):
    head = blk % RET_HEADS
    nb, dk = q_ref.shape
    gam_row = jnp.exp(_log_gamma((nb, dk), head))
    gam_st = jnp.exp(_log_gamma((dk, dk), head))
    q, k, v = q_ref[...], k_ref[...], v_ref[...]
    qt, kt = q.T, k.T
    rows = []
    for r in range(nb):
        R = s_ref[r]
        so_ref[r] = R * gam_st + kt[:, r:r + 1] * v[r:r + 1, :]
        rows.append(jnp.sum(qt[:, r:r + 1] * R, axis=0, keepdims=True))
    qr = jnp.concatenate(rows, axis=0)
    qk = jnp.sum(q * k, axis=-1, keepdims=True)
    o = qk * v + qr * gam_row
    o_ref[...] = _group_norm(o, gr_ref[...]) * sg_ref[...]


def _ret_tail_side(q, k, v_src, sg_src, g_ret, state):
    dk, nb = RET_HEAD_DIM, RET_TAIL_ROWS
    (v, v_blk), (sg, sg_blk) = v_src, sg_src
    row = lambda off: ((nb, dk), lambda i: (i // RET_HEADS, off + i % RET_HEADS))
    st = ((None, nb, None, dk, dk), lambda i: (0, i // RET_HEADS, i % RET_HEADS, 0, 0))
    return _side(
        _ret_tail_block, (DEC_BATCH // nb) * RET_HEADS,
        args=[q, k, v, sg, g_ret.reshape(1, RET_WIDTH), state],
        in_specs=[row(0), row(0), row(v_blk), row(sg_blk),
                  ((1, dk), lambda i: (0, i % RET_HEADS)), st],
        out_specs=[row(0), st],
        out_shape=[jax.ShapeDtypeStruct((M_TAIL, RET_WIDTH), F32),
                   jax.ShapeDtypeStruct(state.shape, F32)])


XATTN_ROWS = 512


def _xattn_main_block(q_ref, k_ref, v_ref, g_ref, o_ref):
    k = k_ref[...].astype(BF16)
    v = v_ref[...].astype(BF16)
    for t in range(q_ref.shape[0] // XATTN_ROWS):
        rs = slice(t * XATTN_ROWS, (t + 1) * XATTN_ROWS)
        sc = lax.dot_general(q_ref[rs, :], k, (((1,), (1,)), ((), ())), preferred_element_type=F32)
        sc = sc * (MEM_HEAD_DIM ** -0.5)
        e = jnp.exp(sc - jnp.max(sc, axis=-1, keepdims=True))
        p = e / jnp.sum(e, axis=-1, keepdims=True)
        om = jnp.dot(p.astype(BF16), v, preferred_element_type=F32)
        o_ref[rs, :] = (om * g_ref[rs, :]).astype(o_ref.dtype)


def _xattn_main_side(cq_src, scg_src, mk, mv):
    dh = MEM_HEAD_DIM
    (cq, cq_blk), (scg, scg_blk) = cq_src, scg_src
    qspec = lambda off: ((SEQ, dh), lambda i: (i // MEM_HEADS, off + i % MEM_HEADS))
    kvspec = ((MEM_LEN, dh), lambda i: (i // MEM_HEADS, i % MEM_HEADS))
    return _side(
        lambda blk, *refs: _xattn_main_block(*refs), BATCH * MEM_HEADS,
        args=[cq, mk, mv, scg],
        in_specs=[qspec(cq_blk), kvspec, kvspec, qspec(scg_blk)],
        out_specs=[qspec(0)],
        out_shape=[jax.ShapeDtypeStruct((M_MAIN, MEM_WIDTH), BF16)])


XATTN_TAIL_ROWS = 4


def _xattn_tail_block(q_ref, g_ref, k_ref, v_ref, o_ref):
    for r in range(q_ref.shape[0]):
        q = q_ref[r]
        sc = jnp.sum(k_ref[r] * q[None], axis=-1, keepdims=True) * (MEM_HEAD_DIM ** -0.5)
        e = jnp.exp(sc - jnp.max(sc, axis=0, keepdims=True))
        o_ref[r] = jnp.sum(e * v_ref[r], axis=0) / jnp.sum(e, axis=0) * g_ref[r]


def _xattn_tail_side(cq, scg, ck, cv):
    nb = XATTN_TAIL_ROWS
    shape3 = (M_TAIL, MEM_HEADS, MEM_HEAD_DIM)
    row = ((nb, MEM_HEADS, MEM_HEAD_DIM), lambda i: (i, 0, 0))
    kv = ((None, nb, MEM_LEN, MEM_HEADS, MEM_HEAD_DIM), lambda i: (0, i, 0, 0, 0))
    return _side(
        lambda blk, *refs: _xattn_tail_block(*refs), DEC_BATCH // nb,
        args=[cq.reshape(shape3), scg.reshape(shape3), ck, cv],
        in_specs=[row, row, kv, kv], out_specs=[row],
        out_shape=[jax.ShapeDtypeStruct(shape3, F32)])


def kernel(x_prompt, x_sample, state_ret, cache_mem_k, cache_mem_v, mem_prompt, g_pre, w_in, g_anorm,
           w_s, b_s, g_ret, g_mem, w_mem_kv, w_out_a, w_out_b, w_out_c, w_out, g_final):
    assert DEPTH == 1 and w_in.shape == (DEPTH, D_MODEL, IN_WIDTH)
    xp = x_prompt.reshape(M_MAIN, D_MODEL)
    xs = x_sample.reshape(M_TAIL, D_MODEL)
    win = w_in.reshape(D_MODEL, IN_WIDTH)

    h_m = _rmsnorm(xp, g_pre[0], 512, BF16)
    h_t = _rmsnorm(xs, g_pre[0], M_TAIL, BF16)
    cos_m, sin_m, cos_t, sin_t = _rope_tables()

    hm = _rmsnorm(mem_prompt.reshape(BATCH * MEM_LEN, D_MODEL), g_mem[0], 512, BF16)
    wkv = w_mem_kv.reshape(D_MODEL, 2 * MEM_WIDTH)
    (mk,) = _wres_matmul([(hm, None, wkv, 0)], [], [(F32, None)], n_cols=MEM_WIDTH, tn=512, tm=512,
                         sub=256, epilogue=_epi_identity, name="mem_k")
    (mv,) = _wres_matmul([(hm, None, wkv, MEM_WIDTH)], [], [(F32, None)], n_cols=MEM_WIDTH, tn=512,
                         tm=512, sub=256, epilogue=_epi_identity, name="mem_v")

    tn = tm = 1024

    def inproj(cols, n_cols, epilogue, out_dtypes, name, tn=tn, exts=(), **kw):
        return _wres_matmul([(h_m, h_t, win, c) for c in cols], list(exts), [out_dtypes],
                            n_cols=n_cols, tn=tn, tm=tm, sub=256, epilogue=epilogue, name=name, **kw)

    second = RET_WIDTH // tn
    id_m, id_t = inproj([((0, OFF_RV), (second, OFF_CQ))], RET_WIDTH + MEM_WIDTH, _epi_identity,
                        (BF16, F32), "inproj_v_cq")
    silu_m, silu_t = inproj([((0, OFF_RG), (second, OFF_CG))], RET_WIDTH + MEM_WIDTH, _epi_silu,
                            (BF16, F32), "inproj_rg_cg")
    gv_m, gv_t = inproj([OFF_AV], A_WIDTH, _epi_gelu, (BF16, F32), "inproj_gv")
    ug_m, ug_t = inproj([OFF_AU, OFF_AG], A_WIDTH, _epi_gelu_silu, (BF16, F32), "inproj_ug", tn=512)

    rope = [(cos_m, cos_t, "rope", 0), (sin_m, sin_t, "rope", 0)]
    q_m, q_t, a_m = inproj(
        [OFF_RQ], RET_WIDTH, functools.partial(_epi_rope, 1.0), (BF16, F32), "inproj_q", exts=rope,
        tn=512, side=_spatial_main_side(ug_m, (gv_m, 0), g_anorm[0], w_s[0], b_s[0]))
    a_t, vn_t = _spatial_tail(ug_t, gv_t, 0, g_anorm[0], w_s[0], b_s[0])
    mem_blk = RET_WIDTH // MEM_HEAD_DIM
    kt_m, k_t, c_m = inproj(
        [OFF_RK], RET_WIDTH, functools.partial(_epi_rope, RET_HEAD_DIM ** -0.5), (BF16, F32),
        "inproj_k", exts=rope, main_transposed=True,
        side=_xattn_main_side((id_m, mem_blk), (silu_m, mem_blk), mk, mv),
        vmem_bytes=VMEM_HOST_BYTES)
    gs_m, gs_t, b_t, ret_s = inproj(
        [OFF_GATES], N_BRANCH * D_MODEL, _epi_sigmoid, (BF16, BF16), "inproj_gates",
        side=_ret_tail_side(q_t, k_t, (id_t, 0), (silu_t, 0), g_ret[0], state_ret),
        vmem_bytes=VMEM_HOST_BYTES)

    b_m, ret_p, c_t3 = _ret_main(
        q_m, kt_m, (id_m, 0), (silu_m, 0), g_ret[0],
        side=_xattn_tail_side(id_t[:, RET_WIDTH:], silu_t[:, RET_WIDTH:], cache_mem_k, cache_mem_v))
    c_t = c_t3.reshape(M_TAIL, MEM_WIDTH)

    merged_m, merged_t = _wres_matmul(
        [(a_m, a_t, w_out_a.reshape(A_WIDTH, D_MODEL), 0),
         (b_m, b_t, w_out_b.reshape(RET_WIDTH, D_MODEL), 0),
         (c_m, c_t, w_out_c.reshape(MEM_WIDTH, D_MODEL), 0)],
        [(gs_m, gs_t, "tile", b * D_MODEL) for b in range(N_BRANCH)],
        [(BF16, BF16)], n_cols=D_MODEL, tn=512, tm=1024, sub=256, epilogue=_epi_merge, name="merge")
    y_m, y_t = _wres_matmul(
        [(merged_m, merged_t, w_out.reshape(D_MODEL, D_MODEL), 0)],
        [(xp, xs, "tile", 0)], [(F32, F32)], n_cols=D_MODEL, tn=512, tm=1024, sub=256,
        epilogue=_epi_residual, name="outproj")
    y_prompt = _rmsnorm(y_m, g_final, 512, F32).reshape(BATCH, SEQ, D_MODEL)
    y_sample = _rmsnorm(y_t, g_final, M_TAIL, F32).reshape(DEC_BATCH, DEC_SEQ, D_MODEL)

    return (y_prompt, y_sample, ret_p,
            mk.reshape(DEPTH, BATCH, MEM_LEN, MEM_HEADS, MEM_HEAD_DIM),
            mv.reshape(DEPTH, BATCH, MEM_LEN, MEM_HEADS, MEM_HEAD_DIM),
            ret_s,
            vn_t.reshape(DEPTH, DEC_BATCH, DEC_SEQ, A_WIDTH))
```

```python
import functools

import jax
import jax.numpy as jnp
from jax import lax
from jax.experimental import pallas as pl
from jax.experimental.pallas import tpu as pltpu

D_MODEL = 4096
BATCH = 4
SEQ = 2048
DEPTH = 1
DEC_BATCH = 128
DEC_SEQ = 1
PAST_LEN = 16384

CHUNK = 128
A_WIDTH = 2048
A_GROUPS = 4
RET_HEADS = 8
RET_HEAD_DIM = 256
RET_WIDTH = RET_HEADS * RET_HEAD_DIM
MEM_LEN = 256
MEM_HEADS = 4
MEM_HEAD_DIM = 256
MEM_WIDTH = MEM_HEADS * MEM_HEAD_DIM
N_BRANCH = 3
ROPE_BASE = 10000.0
EPS = 1e-6

OFF_AU = 0
OFF_AV = OFF_AU + A_WIDTH
OFF_AG = OFF_AV + A_WIDTH
OFF_RQ = OFF_AG + A_WIDTH
OFF_RK = OFF_RQ + RET_WIDTH
OFF_RV = OFF_RK + RET_WIDTH
OFF_RG = OFF_RV + RET_WIDTH
OFF_CQ = OFF_RG + RET_WIDTH
OFF_CG = OFF_CQ + MEM_WIDTH
OFF_GATES = OFF_CG + MEM_WIDTH
IN_WIDTH = OFF_GATES + N_BRANCH * D_MODEL

M_MAIN = BATCH * SEQ
M_TAIL = DEC_BATCH * DEC_SEQ
ROPE_HALF = RET_HEAD_DIM // 2

F32 = jnp.float32
BF16 = jnp.bfloat16

MIB = 1024 * 1024
VMEM_BUDGET_BYTES = 56 * MIB
VMEM_HOST_BYTES = 63 * MIB


def _params(semantics, vmem_bytes=VMEM_BUDGET_BYTES):
    return pltpu.CompilerParams(dimension_semantics=semantics, vmem_limit_bytes=vmem_bytes)


def _rmsnorm_kernel(x_ref, g_ref, o_ref):
    x = x_ref[...].astype(F32)
    y = x * lax.rsqrt(jnp.mean(x * x, axis=-1, keepdims=True) + EPS)
    o_ref[...] = (y * g_ref[...].astype(F32)).astype(o_ref.dtype)


def _rmsnorm(x, g, rows, out_dtype):
    m, d = x.shape
    return pl.pallas_call(
        _rmsnorm_kernel,
        grid=(m // rows,),
        in_specs=[pl.BlockSpec((rows, d), lambda i: (i, 0)),
                  pl.BlockSpec((1, d), lambda i: (0, 0))],
        out_specs=pl.BlockSpec((rows, d), lambda i: (i, 0)),
        out_shape=jax.ShapeDtypeStruct((m, d), out_dtype),
        compiler_params=_params(("arbitrary",)),
        name="rmsnorm",
    )(x, g.reshape(1, d))


def _rope_kernel(cm_ref, sm_ref, ct_ref, st_ref):
    def table(rows, pos):
        j = lax.broadcasted_iota(jnp.int32, (rows, ROPE_HALF), 1).astype(F32)
        inv = ROPE_BASE ** (-j / ROPE_HALF)
        return pos.astype(F32) * inv

    ang = table(SEQ, lax.broadcasted_iota(jnp.int32, (SEQ, ROPE_HALF), 0))
    cm_ref[...] = jnp.cos(ang)
    sm_ref[...] = jnp.sin(ang)
    r = lax.broadcasted_iota(jnp.int32, (M_TAIL, ROPE_HALF), 0)
    t = jnp.zeros_like(r) if DEC_SEQ == 1 else lax.rem(r, DEC_SEQ)
    ang_t = table(M_TAIL, PAST_LEN + t)
    ct_ref[...] = jnp.cos(ang_t)
    st_ref[...] = jnp.sin(ang_t)


def _rope_tables():
    return pl.pallas_call(
        _rope_kernel,
        out_shape=(jax.ShapeDtypeStruct((SEQ, ROPE_HALF), F32),
                   jax.ShapeDtypeStruct((SEQ, ROPE_HALF), F32),
                   jax.ShapeDtypeStruct((M_TAIL, ROPE_HALF), F32),
                   jax.ShapeDtypeStruct((M_TAIL, ROPE_HALF), F32)),
        name="rope_tables",
    )()


CAST_ROWS = 256


def _wres_kernel(*refs, lhs_of, w_cols, n_tiles, n_ext, n_out, mt, has_tail, sub, epilogue,
                 side_fn, n_side_in, n_side_out, main_transposed):
    refs = list(refs)
    n_pairs, n_lhs = len(lhs_of), max(lhs_of) + 1

    def take(k):
        out = refs[:k]
        del refs[:k]
        return out

    lhs_main = take(n_lhs)
    lhs_tail = take(n_lhs) if has_tail else []
    w_hbm = take(n_pairs)
    ext_main = take(n_ext)
    ext_tail = take(n_ext) if has_tail else []
    side_in = take(n_side_in)
    out_main = take(n_out)
    out_tail = take(n_out) if has_tail else []
    side_out = take(n_side_out)
    wf = take(n_pairs)
    wb = take(n_pairs)
    (sem,) = take(1)

    n = pl.program_id(0)
    m = pl.program_id(1)
    tn = wb[0].shape[1]

    def w_copy(p, tile):
        col = w_cols[p][0][1] + tile * tn
        for first_tile, first_col in w_cols[p][1:]:
            col = jnp.where(tile >= first_tile, first_col + (tile - first_tile) * tn, col)
        col = pl.multiple_of(col, tn)
        return pltpu.make_async_copy(w_hbm[p].at[:, pl.ds(col, tn)], wf[p], sem.at[p])

    @pl.when(m == 0)
    def _weights():
        @pl.when(n == 0)
        def _first():
            for p in range(n_pairs):
                w_copy(p, 0).start()

        for p in range(n_pairs):
            w_copy(p, n).wait()

            def body(i, carry, wf_ref=wf[p], wb_ref=wb[p]):
                r = pl.multiple_of(i * CAST_ROWS, CAST_ROWS)
                wb_ref[pl.ds(r, CAST_ROWS), :] = wf_ref[pl.ds(r, CAST_ROWS), :].astype(BF16)
                return carry
            lax.fori_loop(0, wf[p].shape[0] // CAST_ROWS, body, 0)

        @pl.when(n + 1 < n_tiles)
        def _next():
            for p in range(n_pairs):
                w_copy(p, n + 1).start()

    def body(lhs, ext, outs, transposed=False):
        for c in range(0, tn, sub):
            accs = [jnp.dot(lhs[i][...].astype(BF16), b[:, c:c + sub], preferred_element_type=F32)
                    for i, b in zip(lhs_of, wb)]
            res = epilogue(accs, ext, c, sub, n)
            for o, r in zip(outs, res):
                if transposed:
                    o[c:c + sub, :] = r.T.astype(o.dtype)
                else:
                    o[:, c:c + sub] = r.astype(o.dtype)

    body(lhs_main, ext_main, out_main, main_transposed)
    if side_fn is not None:
        side_fn(n, m, *side_in, *side_out)
    if has_tail:
        @pl.when(m == mt - 1)
        def _tail():
            body(lhs_tail, ext_tail, out_tail)


def _wres_matmul(pairs, exts, outs, *, n_cols, tn, tm, sub, epilogue, name, side=None,
                 vmem_bytes=VMEM_BUDGET_BYTES, main_transposed=False):
    m_main = pairs[0][0].shape[0]
    has_tail = pairs[0][1] is not None
    mt = m_main // tm
    n_tiles = n_cols // tn
    grid = (n_tiles, mt)

    lhs, lhs_of = [], []
    for lm, lt, _, _ in pairs:
        ids = [i for i, (a, _) in enumerate(lhs) if a is lm]
        if not ids:
            lhs.append((lm, lt))
        lhs_of.append(ids[0] if ids else len(lhs) - 1)

    in_specs, args = [], []
    for lm, _ in lhs:
        in_specs.append(pl.BlockSpec((tm, lm.shape[1]), lambda n, m: (m, 0)))
        args.append(lm)
    if has_tail:
        for _, lt in lhs:
            in_specs.append(pl.BlockSpec(lt.shape, lambda n, m: (0, 0)))
            args.append(lt)
    w_cols = []
    for _, _, w, cols in pairs:
        segs = ((0, cols),) if isinstance(cols, int) else tuple(cols)
        for i, (first_tile, first_col) in enumerate(segs):
            last_tile = segs[i + 1][0] if i + 1 < len(segs) else n_tiles
            assert first_col % tn == 0 and first_col + (last_tile - first_tile) * tn <= w.shape[1]
        w_cols.append(segs)
        in_specs.append(pl.BlockSpec(memory_space=pl.ANY))
        args.append(w)

    def ext_spec(arr, kind, off, tail):
        rows = arr.shape[0] if tail else tm
        if kind == "tile":
            assert off % tn == 0
            if tail:
                return pl.BlockSpec((rows, tn), lambda n, m, o=off // tn: (0, o + n))
            return pl.BlockSpec((rows, tn), lambda n, m, o=off // tn: (m, o + n))
        assert kind == "rope"
        if tail:
            return pl.BlockSpec(arr.shape, lambda n, m: (0, 0))
        per = arr.shape[0] // tm
        return pl.BlockSpec((tm, arr.shape[1]), lambda n, m: (lax.rem(m, per), 0))

    for em, _, kind, off in exts:
        in_specs.append(ext_spec(em, kind, off, False))
        args.append(em)
    if has_tail:
        for _, et, kind, off in exts:
            in_specs.append(ext_spec(et, kind, off, True))
            args.append(et)

    out_specs, out_shape = [], []
    for dm, _ in outs:
        if main_transposed:
            out_specs.append(pl.BlockSpec((tn, tm), lambda n, m: (n, m)))
            out_shape.append(jax.ShapeDtypeStruct((n_cols, m_main), dm))
        else:
            out_specs.append(pl.BlockSpec((tm, tn), lambda n, m: (m, n)))
            out_shape.append(jax.ShapeDtypeStruct((m_main, n_cols), dm))
    if has_tail:
        m_tail = pairs[0][1].shape[0]
        for _, dt in outs:
            out_specs.append(pl.BlockSpec((m_tail, tn), lambda n, m: (0, n)))
            out_shape.append(jax.ShapeDtypeStruct((m_tail, n_cols), dt))

    if side is not None:
        assert n_tiles * mt >= side["n_blocks"]
        step = lambda n, m: jnp.minimum(n * mt + m, side["n_blocks"] - 1)
        in_specs += [pl.BlockSpec(shape, lambda n, m, f=f: f(step(n, m)))
                     for shape, f in side["in_specs"]]
        args += side["args"]
        out_specs += [pl.BlockSpec(shape, lambda n, m, f=f: f(step(n, m)))
                      for shape, f in side["out_specs"]]
        out_shape += side["out_shape"]
        side_fn = lambda n, m, *refs: side["fn"](step(n, m), *refs)
    else:
        side_fn = None

    scratch = ([pltpu.VMEM((w.shape[0], tn), F32) for _, _, w, _ in pairs]
               + [pltpu.VMEM((w.shape[0], tn), BF16) for _, _, w, _ in pairs]
               + [pltpu.SemaphoreType.DMA((len(pairs),))])
    kern = functools.partial(_wres_kernel, lhs_of=tuple(lhs_of), w_cols=tuple(w_cols),
                             n_tiles=n_tiles, n_ext=len(exts), n_out=len(outs),
                             mt=mt, has_tail=has_tail, sub=sub, epilogue=epilogue,
                             side_fn=side_fn,
                             n_side_in=len(side["args"]) if side else 0,
                             n_side_out=len(side["out_shape"]) if side else 0,
                             main_transposed=main_transposed)
    return pl.pallas_call(
        kern, grid=grid, in_specs=in_specs, out_specs=out_specs, out_shape=out_shape,
        scratch_shapes=scratch, compiler_params=_params(("arbitrary", "arbitrary"), vmem_bytes),
        name=name,
    )(*args)


def _side(fn, n_blocks, args, in_specs, out_specs, out_shape):
    return dict(fn=fn, n_blocks=n_blocks, args=args, in_specs=in_specs, out_specs=out_specs,
                out_shape=out_shape)


def _epi_identity(accs, ext, c, sub, n):
    return [accs[0]]


def _epi_sigmoid(accs, ext, c, sub, n):
    return [jax.nn.sigmoid(accs[0])]


def _epi_gelu_silu(accs, ext, c, sub, n):
    return [jax.nn.gelu(accs[0]) * jax.nn.silu(accs[1])]


def _epi_gelu(accs, ext, c, sub, n):
    return [jax.nn.gelu(accs[0])]


def _epi_silu(accs, ext, c, sub, n):
    return [jax.nn.silu(accs[0])]


def _epi_rope(scale, accs, ext, c, sub, n):
    assert sub == RET_HEAD_DIM
    x = accs[0]
    x1, x2 = x[:, :ROPE_HALF], x[:, ROPE_HALF:]
    cos, sin = ext[0][...], ext[1][...]
    out = jnp.concatenate([x1 * cos - x2 * sin, x1 * sin + x2 * cos], axis=-1)
    return [out if scale == 1.0 else out * scale]


def _epi_merge(accs, ext, c, sub, n):
    g0, g1, g2 = (e[:, c:c + sub] for e in ext)
    return [g0 * accs[0] + g1 * accs[1] + g2 * accs[2]]


def _epi_residual(accs, ext, c, sub, n):
    return [ext[0][:, c:c + sub] + accs[0]]


SPATIAL_ROWS = 256


def _layer_norm(x, g):
    mu = jnp.mean(x, axis=-1, keepdims=True)
    var = jnp.mean(jnp.square(x - mu), axis=-1, keepdims=True)
    return (x - mu) * lax.rsqrt(var + EPS) * g


def _spatial_main_block(ug_ref, gv_ref, ga_ref, ws_ref, bst_ref, o_ref):
    rows = gv_ref.shape[0]
    gw = A_WIDTH // A_GROUPS
    vn = _layer_norm(gv_ref[...].astype(F32), ga_ref[...]).astype(BF16)
    ri = lax.broadcasted_iota(jnp.int32, (CHUNK, CHUNK), 0)
    ci = lax.broadcasted_iota(jnp.int32, (CHUNK, CHUNK), 1)
    for g in range(A_GROUPS):
        w = jnp.where(ri >= ci, ws_ref[g], 0.0).astype(BF16)
        bias = bst_ref[:, g:g + 1]
        for c in range(rows // CHUNK):
            rs = slice(c * CHUNK, (c + 1) * CHUNK)
            cs = slice(g * gw, (g + 1) * gw)
            sp = jnp.dot(w, vn[rs, cs], preferred_element_type=F32) + bias
            o_ref[rs, cs] = (ug_ref[rs, cs] * sp).astype(o_ref.dtype)


def _spatial_main_side(ug, gv_src, g_anorm, w_s, b_s):
    gv, gv_blk = gv_src
    rows = SPATIAL_ROWS
    return _side(
        lambda blk, *refs: _spatial_main_block(*refs), M_MAIN // rows,
        args=[ug, gv, g_anorm.reshape(1, A_WIDTH), w_s, b_s.T],
        in_specs=[((rows, A_WIDTH), lambda i: (i, 0)),
                  ((rows, A_WIDTH), lambda i: (i, gv_blk)),
                  ((1, A_WIDTH), lambda i: (0, 0)),
                  ((A_GROUPS, CHUNK, CHUNK), lambda i: (0, 0, 0)),
                  ((CHUNK, A_GROUPS), lambda i: (0, 0))],
        out_specs=[((rows, A_WIDTH), lambda i: (i, 0))],
        out_shape=[jax.ShapeDtypeStruct((M_MAIN, A_WIDTH), BF16)])


def _spatial_tail_kernel(ug_ref, gv_ref, ga_ref, ws_ref, bs_ref, o_ref, vn_ref, *, gv_col):
    gw = A_WIDTH // A_GROUPS
    vn = _layer_norm(gv_ref[:, gv_col:gv_col + A_WIDTH], ga_ref[...])
    vn_ref[...] = vn
    for g in range(A_GROUPS):
        cs = slice(g * gw, (g + 1) * gw)
        sp = vn[:, cs] * ws_ref[g, 0:1, 0:1] + bs_ref[g:g + 1, 0:1]
        o_ref[:, cs] = (ug_ref[:, cs] * sp).astype(o_ref.dtype)


def _spatial_tail(ug, gv, gv_col, g_anorm, w_s, b_s):
    m = ug.shape[0]
    return pl.pallas_call(
        functools.partial(_spatial_tail_kernel, gv_col=gv_col),
        out_shape=(jax.ShapeDtypeStruct((m, A_WIDTH), BF16),
                   jax.ShapeDtypeStruct((m, A_WIDTH), F32)),
        compiler_params=_params(None),
        name="spatial_tail",
    )(ug, gv, g_anorm.reshape(1, A_WIDTH), w_s, b_s)


def _log_gamma(shape, head):
    return jnp.log1p(-jnp.exp2(jnp.full(shape, -5.0, F32) - head.astype(F32)))


def _group_norm(o, g):
    mu = jnp.mean(o, axis=-1, keepdims=True)
    var = jnp.mean(jnp.square(o - mu), axis=-1, keepdims=True)
    return (o - mu) * lax.rsqrt(var + EPS) * g


def _ret_main_kernel(*refs, side_fn, n_side_in):
    q_ref, kt_ref, v_ref, sg_ref, gr_ref = refs[:5]
    side_in = refs[5:5 + n_side_in]
    o_ref, r_ref = refs[5 + n_side_in:7 + n_side_in]
    side_out = refs[7 + n_side_in:]
    head = pl.program_id(1)
    L, dk = CHUNK, RET_HEAD_DIM
    ri = lax.broadcasted_iota(jnp.int32, (L, L), 0).astype(F32)
    ci = lax.broadcasted_iota(jnp.int32, (L, L), 1).astype(F32)
    diff = ri - ci
    decay_in = jnp.where(diff >= 0, jnp.exp(_log_gamma((L, L), head) * jnp.maximum(diff, 0.0)), 0.0)
    rw = lax.broadcasted_iota(jnp.int32, (L, dk), 0).astype(F32)
    decay_q = jnp.exp(_log_gamma((L, dk), head) * (rw + 1.0))
    cw = lax.broadcasted_iota(jnp.int32, (dk, L), 1).astype(F32)
    decay_kt = jnp.exp(_log_gamma((dk, L), head) * (L - 1.0 - cw))
    decay_blk = jnp.exp(_log_gamma((dk, dk), head) * L)
    gr = gr_ref[...]

    R = jnp.zeros((dk, dk), F32)
    for i in range(SEQ // L):
        rs = slice(i * L, (i + 1) * L)
        qi, kti, vi = q_ref[rs, :], kt_ref[:, rs], v_ref[rs, :]
        s = jnp.dot(qi, kti, preferred_element_type=F32) * decay_in
        o = (jnp.dot(s.astype(BF16), vi, preferred_element_type=F32)
             + jnp.dot(qi, R.astype(BF16), preferred_element_type=F32) * decay_q)
        kdt = (kti.astype(F32) * decay_kt).astype(BF16)
        R = R * decay_blk + jnp.dot(kdt, vi, preferred_element_type=F32)
        o_ref[rs, :] = (_group_norm(o, gr) * sg_ref[rs, :]).astype(o_ref.dtype)
    r_ref[...] = R
    side_fn(pl.program_id(0) * RET_HEADS + head, *side_in, *side_out)


def _ret_main(q, kt, v_src, sg_src, g_ret, side):
    dk = RET_HEAD_DIM
    (v, v_blk), (sg, sg_blk) = v_src, sg_src
    blk = lambda off: pl.BlockSpec((SEQ, dk), lambda b, h: (b, off + h))
    step = lambda b, h: b * RET_HEADS + h
    assert BATCH * RET_HEADS == side["n_blocks"]
    in_specs = [blk(0), pl.BlockSpec((dk, SEQ), lambda b, h: (h, b)), blk(v_blk), blk(sg_blk),
                pl.BlockSpec((1, dk), lambda b, h: (0, h))]
    in_specs += [pl.BlockSpec(shape, lambda b, h, f=f: f(step(b, h))) for shape, f in side["in_specs"]]
    out_specs = [blk(0), pl.BlockSpec((None, None, None, dk, dk), lambda b, h: (0, b, h, 0, 0))]
    out_specs += [pl.BlockSpec(shape, lambda b, h, f=f: f(step(b, h))) for shape, f in side["out_specs"]]
    out_shape = [jax.ShapeDtypeStruct((M_MAIN, RET_WIDTH), BF16),
                 jax.ShapeDtypeStruct((DEPTH, BATCH, RET_HEADS, dk, dk), F32)] + side["out_shape"]
    kern = functools.partial(_ret_main_kernel, side_fn=side["fn"], n_side_in=len(side["args"]))
    return pl.pallas_call(
        kern, grid=(BATCH, RET_HEADS), in_specs=in_specs, out_specs=out_specs, out_shape=out_shape,
        compiler_params=_params(("arbitrary", "arbitrary")), name="ret_main",
    )(q, kt, v, sg, g_ret.reshape(1, RET_WIDTH), *side["args"])


RET_TAIL_ROWS = 16


def _ret_tail_block(blk, q_ref, k_ref, v_ref, sg_ref, gr_ref, s_ref, o_ref, so_ref):
    head = blk % RET_HEADS
    nb, dk = q_ref.shape
    gam_row = jnp.exp(_log_gamma((nb, dk), head))
    gam_st = jnp.exp(_log_gamma((dk, dk), head))
    q, k, v = q_ref[...], k_ref[...], v_ref[...]
    qt, kt = q.T, k.T
    rows = []
    for r in range(nb):
        R = s_ref[r]
        so_ref[r] = R * gam_st + kt[:, r:r + 1] * v[r:r + 1, :]
        rows.append(jnp.sum(qt[:, r:r + 1] * R, axis=0, keepdims=True))
    qr = jnp.concatenate(rows, axis=0)
    qk = jnp.sum(q * k, axis=-1, keepdims=True)
    o = qk * v + qr * gam_row
    o_ref[...] = _group_norm(o, gr_ref[...]) * sg_ref[...]


def _ret_tail_side(q, k, v_src, sg_src, g_ret, state):
    dk, nb = RET_HEAD_DIM, RET_TAIL_ROWS
    (v, v_blk), (sg, sg_blk) = v_src, sg_src
    row = lambda off: ((nb, dk), lambda i: (i // RET_HEADS, off + i % RET_HEADS))
    st = ((None, nb, None, dk, dk), lambda i: (0, i // RET_HEADS, i % RET_HEADS, 0, 0))
    return _side(
        _ret_tail_block, (DEC_BATCH // nb) * RET_HEADS,
        args=[q, k, v, sg, g_ret.reshape(1, RET_WIDTH), state],
        in_specs=[row(0), row(0), row(v_blk), row(sg_blk),
                  ((1, dk), lambda i: (0, i % RET_HEADS)), st],
        out_specs=[row(0), st],
        out_shape=[jax.ShapeDtypeStruct((M_TAIL, RET_WIDTH), F32),
                   jax.ShapeDtypeStruct(state.shape, F32)])


XATTN_ROWS = 512


def _xattn_main_block(q_ref, k_ref, v_ref, g_ref, o_ref):
    k = k_ref[...].astype(BF16)
    v = v_ref[...].astype(BF16)
    for t in range(q_ref.shape[0] // XATTN_ROWS):
        rs = slice(t * XATTN_ROWS, (t + 1) * XATTN_ROWS)
        sc = lax.dot_general(q_ref[rs, :], k, (((1,), (1,)), ((), ())), preferred_element_type=F32)
        sc = sc * (MEM_HEAD_DIM ** -0.5)
        e = jnp.exp(sc - jnp.max(sc, axis=-1, keepdims=True))
        p = e / jnp.sum(e, axis=-1, keepdims=True)
        om = jnp.dot(p.astype(BF16), v, preferred_element_type=F32)
        o_ref[rs, :] = (om * g_ref[rs, :]).astype(o_ref.dtype)


def _xattn_main_side(cq_src, scg_src, kv):
    dh = MEM_HEAD_DIM
    (cq, cq_blk), (scg, scg_blk) = cq_src, scg_src
    qspec = lambda off: ((SEQ, dh), lambda i: (i // MEM_HEADS, off + i % MEM_HEADS))
    kvspec = lambda off: ((MEM_LEN, dh), lambda i: (i // MEM_HEADS, off + i % MEM_HEADS))
    return _side(
        lambda blk, *refs: _xattn_main_block(*refs), BATCH * MEM_HEADS,
        args=[cq, kv, kv, scg],
        in_specs=[qspec(cq_blk), kvspec(0), kvspec(MEM_HEADS), qspec(scg_blk)],
        out_specs=[qspec(0)],
        out_shape=[jax.ShapeDtypeStruct((M_MAIN, MEM_WIDTH), BF16)])


XATTN_TAIL_ROWS = 4


def _xattn_tail_block(q_ref, g_ref, k_ref, v_ref, o_ref):
    for r in range(q_ref.shape[0]):
        q = q_ref[r]
        sc = jnp.sum(k_ref[r] * q[None], axis=-1, keepdims=True) * (MEM_HEAD_DIM ** -0.5)
        e = jnp.exp(sc - jnp.max(sc, axis=0, keepdims=True))
        o_ref[r] = jnp.sum(e * v_ref[r], axis=0) / jnp.sum(e, axis=0) * g_ref[r]


def _xattn_tail_side(cq, scg, ck, cv):
    nb = XATTN_TAIL_ROWS
    shape3 = (M_TAIL, MEM_HEADS, MEM_HEAD_DIM)
    row = ((nb, MEM_HEADS, MEM_HEAD_DIM), lambda i: (i, 0, 0))
    kv = ((None, nb, MEM_LEN, MEM_HEADS, MEM_HEAD_DIM), lambda i: (0, i, 0, 0, 0))
    return _side(
        lambda blk, *refs: _xattn_tail_block(*refs), DEC_BATCH // nb,
        args=[cq.reshape(shape3), scg.reshape(shape3), ck, cv],
        in_specs=[row, row, kv, kv], out_specs=[row],
        out_shape=[jax.ShapeDtypeStruct(shape3, F32)])


def kernel(x_prompt, x_sample, state_ret, cache_mem_k, cache_mem_v, mem_prompt, g_pre, w_in, g_anorm,
           w_s, b_s, g_ret, g_mem, w_mem_kv, w_out_a, w_out_b, w_out_c, w_out, g_final):
    assert DEPTH == 1 and w_in.shape == (DEPTH, D_MODEL, IN_WIDTH)
    xp = x_prompt.reshape(M_MAIN, D_MODEL)
    xs = x_sample.reshape(M_TAIL, D_MODEL)
    win = w_in.reshape(D_MODEL, IN_WIDTH)

    h_m = _rmsnorm(xp, g_pre[0], 512, BF16)
    h_t = _rmsnorm(xs, g_pre[0], M_TAIL, BF16)
    cos_m, sin_m, cos_t, sin_t = _rope_tables()

    hm = _rmsnorm(mem_prompt.reshape(BATCH * MEM_LEN, D_MODEL), g_mem[0], 512, BF16)
    wkv = w_mem_kv.reshape(D_MODEL, 2 * MEM_WIDTH)
    (kv,) = _wres_matmul([(hm, None, wkv, 0)], [], [(F32, None)], n_cols=2 * MEM_WIDTH, tn=1024,
                         tm=512, sub=256, epilogue=_epi_identity, name="mem_kv")

    tn = tm = 1024

    def inproj(cols, n_cols, epilogue, out_dtypes, name, tn=tn, exts=(), **kw):
        return _wres_matmul([(h_m, h_t, win, c) for c in cols], list(exts), [out_dtypes],
                            n_cols=n_cols, tn=tn, tm=tm, sub=256, epilogue=epilogue, name=name, **kw)

    second = RET_WIDTH // tn
    id_m, id_t = inproj([((0, OFF_RV), (second, OFF_CQ))], RET_WIDTH + MEM_WIDTH, _epi_identity,
                        (BF16, F32), "inproj_v_cq")
    silu_m, silu_t = inproj([((0, OFF_RG), (second, OFF_CG))], RET_WIDTH + MEM_WIDTH, _epi_silu,
                            (BF16, F32), "inproj_rg_cg")
    gv_m, gv_t = inproj([OFF_AV], A_WIDTH, _epi_gelu, (BF16, F32), "inproj_gv")
    ug_m, ug_t = inproj([OFF_AU, OFF_AG], A_WIDTH, _epi_gelu_silu, (BF16, F32), "inproj_ug", tn=512)

    rope = [(cos_m, cos_t, "rope", 0), (sin_m, sin_t, "rope", 0)]
    q_m, q_t, a_m = inproj(
        [OFF_RQ], RET_WIDTH, functools.partial(_epi_rope, 1.0), (BF16, F32), "inproj_q", exts=rope,
        tn=512, side=_spatial_main_side(ug_m, (gv_m, 0), g_anorm[0], w_s[0], b_s[0]))
    a_t, vn_t = _spatial_tail(ug_t, gv_t, 0, g_anorm[0], w_s[0], b_s[0])
    mem_blk = RET_WIDTH // MEM_HEAD_DIM
    kt_m, k_t, c_m = inproj(
        [OFF_RK], RET_WIDTH, functools.partial(_epi_rope, RET_HEAD_DIM ** -0.5), (BF16, F32),
        "inproj_k", exts=rope, main_transposed=True,
        side=_xattn_main_side((id_m, mem_blk), (silu_m, mem_blk), kv),
        vmem_bytes=VMEM_HOST_BYTES)
    gs_m, gs_t, b_t, ret_s = inproj(
        [OFF_GATES], N_BRANCH * D_MODEL, _epi_sigmoid, (BF16, BF16), "inproj_gates",
        side=_ret_tail_side(q_t, k_t, (id_t, 0), (silu_t, 0), g_ret[0], state_ret),
        vmem_bytes=VMEM_HOST_BYTES)

    b_m, ret_p, c_t3 = _ret_main(
        q_m, kt_m, (id_m, 0), (silu_m, 0), g_ret[0],
        side=_xattn_tail_side(id_t[:, RET_WIDTH:], silu_t[:, RET_WIDTH:], cache_mem_k, cache_mem_v))
    c_t = c_t3.reshape(M_TAIL, MEM_WIDTH)

    merged_m, merged_t = _wres_matmul(
        [(a_m, a_t, w_out_a.reshape(A_WIDTH, D_MODEL), 0),
         (b_m, b_t, w_out_b.reshape(RET_WIDTH, D_MODEL), 0),
         (c_m, c_t, w_out_c.reshape(MEM_WIDTH, D_MODEL), 0)],
        [(gs_m, gs_t, "tile", b * D_MODEL) for b in range(N_BRANCH)],
        [(BF16, BF16)], n_cols=D_MODEL, tn=512, tm=1024, sub=256, epilogue=_epi_merge, name="merge")
    y_m, y_t = _wres_matmul(
        [(merged_m, merged_t, w_out.reshape(D_MODEL, D_MODEL), 0)],
        [(xp, xs, "tile", 0)], [(F32, F32)], n_cols=D_MODEL, tn=1024, tm=1024, sub=256,
        epilogue=_epi_residual, name="outproj", vmem_bytes=VMEM_HOST_BYTES)
    y_prompt = _rmsnorm(y_m, g_final, 512, F32).reshape(BATCH, SEQ, D_MODEL)
    y_sample = _rmsnorm(y_t, g_final, M_TAIL, F32).reshape(DEC_BATCH, DEC_SEQ, D_MODEL)

    mem_shape = (DEPTH, BATCH, MEM_LEN, MEM_HEADS, MEM_HEAD_DIM)
    return (y_prompt, y_sample, ret_p,
            kv[:, :MEM_WIDTH].reshape(mem_shape), kv[:, MEM_WIDTH:].reshape(mem_shape),
            ret_s,
            vn_t.reshape(DEPTH, DEC_BATCH, DEC_SEQ, A_WIDTH))
```

```python
import functools

import jax
import jax.numpy as jnp
from jax import lax
from jax.experimental import pallas as pl
from jax.experimental.pallas import tpu as pltpu

D_MODEL = 4096
BATCH = 4
SEQ = 2048
DEPTH = 1
DEC_BATCH = 128
DEC_SEQ = 1
PAST_LEN = 16384

CHUNK = 128
A_WIDTH = 2048
A_GROUPS = 4
RET_HEADS = 8
RET_HEAD_DIM = 256
RET_WIDTH = RET_HEADS * RET_HEAD_DIM
MEM_LEN = 256
MEM_HEADS = 4
MEM_HEAD_DIM = 256
MEM_WIDTH = MEM_HEADS * MEM_HEAD_DIM
N_BRANCH = 3
ROPE_BASE = 10000.0
EPS = 1e-6

OFF_AU = 0
OFF_AV = OFF_AU + A_WIDTH
OFF_AG = OFF_AV + A_WIDTH
OFF_RQ = OFF_AG + A_WIDTH
OFF_RK = OFF_RQ + RET_WIDTH
OFF_RV = OFF_RK + RET_WIDTH
OFF_RG = OFF_RV + RET_WIDTH
OFF_CQ = OFF_RG + RET_WIDTH
OFF_CG = OFF_CQ + MEM_WIDTH
OFF_GATES = OFF_CG + MEM_WIDTH
IN_WIDTH = OFF_GATES + N_BRANCH * D_MODEL

M_MAIN = BATCH * SEQ
M_TAIL = DEC_BATCH * DEC_SEQ
ROPE_HALF = RET_HEAD_DIM // 2

F32 = jnp.float32
BF16 = jnp.bfloat16

MIB = 1024 * 1024
VMEM_BUDGET_BYTES = 56 * MIB
VMEM_HOST_BYTES = 63 * MIB


def _params(semantics, vmem_bytes=VMEM_BUDGET_BYTES):
    return pltpu.CompilerParams(dimension_semantics=semantics, vmem_limit_bytes=vmem_bytes)


def _rmsnorm_kernel(x_ref, g_ref, o_ref):
    x = x_ref[...].astype(F32)
    y = x * lax.rsqrt(jnp.mean(x * x, axis=-1, keepdims=True) + EPS)
    o_ref[...] = (y * g_ref[...].astype(F32)).astype(o_ref.dtype)


def _rmsnorm(x, g, rows, out_dtype):
    m, d = x.shape
    return pl.pallas_call(
        _rmsnorm_kernel,
        grid=(m // rows,),
        in_specs=[pl.BlockSpec((rows, d), lambda i: (i, 0)),
                  pl.BlockSpec((1, d), lambda i: (0, 0))],
        out_specs=pl.BlockSpec((rows, d), lambda i: (i, 0)),
        out_shape=jax.ShapeDtypeStruct((m, d), out_dtype),
        compiler_params=_params(("arbitrary",)),
        name="rmsnorm",
    )(x, g.reshape(1, d))


def _rope_kernel(cm_ref, sm_ref, ct_ref, st_ref):
    def table(rows, pos):
        j = lax.broadcasted_iota(jnp.int32, (rows, ROPE_HALF), 1).astype(F32)
        inv = ROPE_BASE ** (-j / ROPE_HALF)
        return pos.astype(F32) * inv

    ang = table(SEQ, lax.broadcasted_iota(jnp.int32, (SEQ, ROPE_HALF), 0))
    cm_ref[...] = jnp.cos(ang)
    sm_ref[...] = jnp.sin(ang)
    r = lax.broadcasted_iota(jnp.int32, (M_TAIL, ROPE_HALF), 0)
    t = jnp.zeros_like(r) if DEC_SEQ == 1 else lax.rem(r, DEC_SEQ)
    ang_t = table(M_TAIL, PAST_LEN + t)
    ct_ref[...] = jnp.cos(ang_t)
    st_ref[...] = jnp.sin(ang_t)


def _rope_tables():
    return pl.pallas_call(
        _rope_kernel,
        out_shape=(jax.ShapeDtypeStruct((SEQ, ROPE_HALF), F32),
                   jax.ShapeDtypeStruct((SEQ, ROPE_HALF), F32),
                   jax.ShapeDtypeStruct((M_TAIL, ROPE_HALF), F32),
                   jax.ShapeDtypeStruct((M_TAIL, ROPE_HALF), F32)),
        name="rope_tables",
    )()


CAST_ROWS = 256


LHS_RING_SLOTS = 3


def _wres_kernel(*refs, lhs_of, w_cols, n_tiles, n_ext, n_out, mt, has_tail, sub, epilogue,
                 side_fn, n_side_in, n_side_out, main_transposed, lhs_ring):
    refs = list(refs)
    n_pairs, n_lhs = len(lhs_of), max(lhs_of) + 1

    def take(k):
        out = refs[:k]
        del refs[:k]
        return out

    lhs_main = take(n_lhs)
    lhs_tail = take(n_lhs) if has_tail else []
    w_hbm = take(n_pairs)
    ext_main = take(n_ext)
    ext_tail = take(n_ext) if has_tail else []
    side_in = take(n_side_in)
    out_main = take(n_out)
    out_tail = take(n_out) if has_tail else []
    side_out = take(n_side_out)
    wf = take(n_pairs)
    wb = take(n_pairs)
    (sem,) = take(1)
    ring = take(n_lhs) if lhs_ring else []
    ring_sem = take(1)[0] if lhs_ring else None

    n = pl.program_id(0)
    m = pl.program_id(1)
    tn = wb[0].shape[1]

    if lhs_ring:
        step, n_steps = n * mt + m, n_tiles * mt

        def lhs_copy(i, s):
            slot = lax.rem(s, LHS_RING_SLOTS)
            rows = pl.multiple_of(lax.rem(s, mt) * ring[i].shape[1], ring[i].shape[1])
            return pltpu.make_async_copy(lhs_main[i].at[pl.ds(rows, ring[i].shape[1]), :],
                                         ring[i].at[slot], ring_sem.at[i, slot])

        @pl.when(step == 0)
        def _fill():
            for i in range(n_lhs):
                for s in range(min(LHS_RING_SLOTS - 1, n_steps)):
                    lhs_copy(i, s).start()

        @pl.when(step + LHS_RING_SLOTS - 1 < n_steps)
        def _ahead():
            for i in range(n_lhs):
                lhs_copy(i, step + LHS_RING_SLOTS - 1).start()

        for i in range(n_lhs):
            lhs_copy(i, step).wait()
        lhs_main = [ring[i].at[lax.rem(step, LHS_RING_SLOTS)] for i in range(n_lhs)]

    def w_copy(p, tile):
        col = w_cols[p][0][1] + tile * tn
        for first_tile, first_col in w_cols[p][1:]:
            col = jnp.where(tile >= first_tile, first_col + (tile - first_tile) * tn, col)
        col = pl.multiple_of(col, tn)
        return pltpu.make_async_copy(w_hbm[p].at[:, pl.ds(col, tn)], wf[p], sem.at[p])

    @pl.when(m == 0)
    def _weights():
        @pl.when(n == 0)
        def _first():
            for p in range(n_pairs):
                w_copy(p, 0).start()

        for p in range(n_pairs):
            w_copy(p, n).wait()

            def body(i, carry, wf_ref=wf[p], wb_ref=wb[p]):
                r = pl.multiple_of(i * CAST_ROWS, CAST_ROWS)
                wb_ref[pl.ds(r, CAST_ROWS), :] = wf_ref[pl.ds(r, CAST_ROWS), :].astype(BF16)
                return carry
            lax.fori_loop(0, wf[p].shape[0] // CAST_ROWS, body, 0)

        @pl.when(n + 1 < n_tiles)
        def _next():
            for p in range(n_pairs):
                w_copy(p, n + 1).start()

    def body(lhs, ext, outs, transposed=False):
        for c in range(0, tn, sub):
            accs = [jnp.dot(lhs[i][...].astype(BF16), b[:, c:c + sub], preferred_element_type=F32)
                    for i, b in zip(lhs_of, wb)]
            res = epilogue(accs, ext, c, sub, n)
            for o, r in zip(outs, res):
                if transposed:
                    o[c:c + sub, :] = r.T.astype(o.dtype)
                else:
                    o[:, c:c + sub] = r.astype(o.dtype)

    body(lhs_main, ext_main, out_main, main_transposed)
    if side_fn is not None:
        side_fn(n, m, *side_in, *side_out)
    if has_tail:
        @pl.when(m == mt - 1)
        def _tail():
            body(lhs_tail, ext_tail, out_tail)


def _wres_matmul(pairs, exts, outs, *, n_cols, tn, tm, sub, epilogue, name, side=None,
                 vmem_bytes=VMEM_BUDGET_BYTES, main_transposed=False, lhs_ring=False):
    m_main = pairs[0][0].shape[0]
    has_tail = pairs[0][1] is not None
    mt = m_main // tm
    n_tiles = n_cols // tn
    grid = (n_tiles, mt)

    lhs, lhs_of = [], []
    for lm, lt, _, _ in pairs:
        ids = [i for i, (a, _) in enumerate(lhs) if a is lm]
        if not ids:
            lhs.append((lm, lt))
        lhs_of.append(ids[0] if ids else len(lhs) - 1)

    in_specs, args = [], []
    for lm, _ in lhs:
        in_specs.append(pl.BlockSpec(memory_space=pl.ANY) if lhs_ring else
                        pl.BlockSpec((tm, lm.shape[1]), lambda n, m: (m, 0)))
        args.append(lm)
    if has_tail:
        for _, lt in lhs:
            in_specs.append(pl.BlockSpec(lt.shape, lambda n, m: (0, 0)))
            args.append(lt)
    w_cols = []
    for _, _, w, cols in pairs:
        segs = ((0, cols),) if isinstance(cols, int) else tuple(cols)
        for i, (first_tile, first_col) in enumerate(segs):
            last_tile = segs[i + 1][0] if i + 1 < len(segs) else n_tiles
            assert first_col % tn == 0 and first_col + (last_tile - first_tile) * tn <= w.shape[1]
        w_cols.append(segs)
        in_specs.append(pl.BlockSpec(memory_space=pl.ANY))
        args.append(w)

    def ext_spec(arr, kind, off, tail):
        rows = arr.shape[0] if tail else tm
        if kind == "tile":
            assert off % tn == 0
            if tail:
                return pl.BlockSpec((rows, tn), lambda n, m, o=off // tn: (0, o + n))
            return pl.BlockSpec((rows, tn), lambda n, m, o=off // tn: (m, o + n))
        assert kind == "rope"
        if tail:
            return pl.BlockSpec(arr.shape, lambda n, m: (0, 0))
        per = arr.shape[0] // tm
        return pl.BlockSpec((tm, arr.shape[1]), lambda n, m: (lax.rem(m, per), 0))

    for em, _, kind, off in exts:
        in_specs.append(ext_spec(em, kind, off, False))
        args.append(em)
    if has_tail:
        for _, et, kind, off in exts:
            in_specs.append(ext_spec(et, kind, off, True))
            args.append(et)

    out_specs, out_shape = [], []
    for dm, _ in outs:
        if main_transposed:
            out_specs.append(pl.BlockSpec((tn, tm), lambda n, m: (n, m)))
            out_shape.append(jax.ShapeDtypeStruct((n_cols, m_main), dm))
        else:
            out_specs.append(pl.BlockSpec((tm, tn), lambda n, m: (m, n)))
            out_shape.append(jax.ShapeDtypeStruct((m_main, n_cols), dm))
    if has_tail:
        m_tail = pairs[0][1].shape[0]
        for _, dt in outs:
            out_specs.append(pl.BlockSpec((m_tail, tn), lambda n, m: (0, n)))
            out_shape.append(jax.ShapeDtypeStruct((m_tail, n_cols), dt))

    if side is not None:
        assert n_tiles * mt >= side["n_blocks"]
        step = lambda n, m: jnp.minimum(n * mt + m, side["n_blocks"] - 1)
        in_specs += [pl.BlockSpec(shape, lambda n, m, f=f: f(step(n, m)))
                     for shape, f in side["in_specs"]]
        args += side["args"]
        out_specs += [pl.BlockSpec(shape, lambda n, m, f=f: f(step(n, m)))
                      for shape, f in side["out_specs"]]
        out_shape += side["out_shape"]
        side_fn = lambda n, m, *refs: side["fn"](step(n, m), *refs)
    else:
        side_fn = None

    scratch = ([pltpu.VMEM((w.shape[0], tn), F32) for _, _, w, _ in pairs]
               + [pltpu.VMEM((w.shape[0], tn), BF16) for _, _, w, _ in pairs]
               + [pltpu.SemaphoreType.DMA((len(pairs),))])
    if lhs_ring:
        scratch += [pltpu.VMEM((LHS_RING_SLOTS, tm, lm.shape[1]), lm.dtype) for lm, _ in lhs]
        scratch += [pltpu.SemaphoreType.DMA((len(lhs), LHS_RING_SLOTS))]
    kern = functools.partial(_wres_kernel, lhs_of=tuple(lhs_of), w_cols=tuple(w_cols),
                             n_tiles=n_tiles, n_ext=len(exts), n_out=len(outs),
                             mt=mt, has_tail=has_tail, sub=sub, epilogue=epilogue,
                             side_fn=side_fn,
                             n_side_in=len(side["args"]) if side else 0,
                             n_side_out=len(side["out_shape"]) if side else 0,
                             main_transposed=main_transposed, lhs_ring=lhs_ring)
    return pl.pallas_call(
        kern, grid=grid, in_specs=in_specs, out_specs=out_specs, out_shape=out_shape,
        scratch_shapes=scratch, compiler_params=_params(("arbitrary", "arbitrary"), vmem_bytes),
        name=name,
    )(*args)


def _side(fn, n_blocks, args, in_specs, out_specs, out_shape):
    return dict(fn=fn, n_blocks=n_blocks, args=args, in_specs=in_specs, out_specs=out_specs,
                out_shape=out_shape)


def _epi_identity(accs, ext, c, sub, n):
    return [accs[0]]


def _epi_sigmoid(accs, ext, c, sub, n):
    return [jax.nn.sigmoid(accs[0])]


def _epi_gelu_silu(accs, ext, c, sub, n):
    return [jax.nn.gelu(accs[0]) * jax.nn.silu(accs[1])]


def _epi_gelu(accs, ext, c, sub, n):
    return [jax.nn.gelu(accs[0])]


def _epi_silu(accs, ext, c, sub, n):
    return [jax.nn.silu(accs[0])]


def _epi_rope(scale, accs, ext, c, sub, n):
    assert sub == RET_HEAD_DIM
    x = accs[0]
    x1, x2 = x[:, :ROPE_HALF], x[:, ROPE_HALF:]
    cos, sin = ext[0][...], ext[1][...]
    out = jnp.concatenate([x1 * cos - x2 * sin, x1 * sin + x2 * cos], axis=-1)
    return [out if scale == 1.0 else out * scale]


def _epi_merge(accs, ext, c, sub, n):
    g0, g1, g2 = (e[:, c:c + sub] for e in ext)
    return [g0 * accs[0] + g1 * accs[1] + g2 * accs[2]]


def _epi_residual(accs, ext, c, sub, n):
    return [ext[0][:, c:c + sub] + accs[0]]


SPATIAL_ROWS = 256


def _layer_norm(x, g):
    mu = jnp.mean(x, axis=-1, keepdims=True)
    var = jnp.mean(jnp.square(x - mu), axis=-1, keepdims=True)
    return (x - mu) * lax.rsqrt(var + EPS) * g


def _spatial_main_block(ug_ref, gv_ref, ga_ref, ws_ref, bst_ref, o_ref):
    rows = gv_ref.shape[0]
    gw = A_WIDTH // A_GROUPS
    vn = _layer_norm(gv_ref[...].astype(F32), ga_ref[...]).astype(BF16)
    ri = lax.broadcasted_iota(jnp.int32, (CHUNK, CHUNK), 0)
    ci = lax.broadcasted_iota(jnp.int32, (CHUNK, CHUNK), 1)
    for g in range(A_GROUPS):
        w = jnp.where(ri >= ci, ws_ref[g], 0.0).astype(BF16)
        bias = bst_ref[:, g:g + 1]
        for c in range(rows // CHUNK):
            rs = slice(c * CHUNK, (c + 1) * CHUNK)
            cs = slice(g * gw, (g + 1) * gw)
            sp = jnp.dot(w, vn[rs, cs], preferred_element_type=F32) + bias
            o_ref[rs, cs] = (ug_ref[rs, cs] * sp).astype(o_ref.dtype)


def _spatial_main_side(ug, gv_src, g_anorm, w_s, b_s):
    gv, gv_blk = gv_src
    rows = SPATIAL_ROWS
    return _side(
        lambda blk, *refs: _spatial_main_block(*refs), M_MAIN // rows,
        args=[ug, gv, g_anorm.reshape(1, A_WIDTH), w_s, b_s.T],
        in_specs=[((rows, A_WIDTH), lambda i: (i, 0)),
                  ((rows, A_WIDTH), lambda i: (i, gv_blk)),
                  ((1, A_WIDTH), lambda i: (0, 0)),
                  ((A_GROUPS, CHUNK, CHUNK), lambda i: (0, 0, 0)),
                  ((CHUNK, A_GROUPS), lambda i: (0, 0))],
        out_specs=[((rows, A_WIDTH), lambda i: (i, 0))],
        out_shape=[jax.ShapeDtypeStruct((M_MAIN, A_WIDTH), BF16)])


def _spatial_tail_kernel(ug_ref, gv_ref, ga_ref, ws_ref, bs_ref, o_ref, vn_ref, *, gv_col):
    gw = A_WIDTH // A_GROUPS
    vn = _layer_norm(gv_ref[:, gv_col:gv_col + A_WIDTH], ga_ref[...])
    vn_ref[...] = vn
    for g in range(A_GROUPS):
        cs = slice(g * gw, (g + 1) * gw)
        sp = vn[:, cs] * ws_ref[g, 0:1, 0:1] + bs_ref[g:g + 1, 0:1]
        o_ref[:, cs] = (ug_ref[:, cs] * sp).astype(o_ref.dtype)


def _spatial_tail(ug, gv, gv_col, g_anorm, w_s, b_s):
    m = ug.shape[0]
    return pl.pallas_call(
        functools.partial(_spatial_tail_kernel, gv_col=gv_col),
        out_shape=(jax.ShapeDtypeStruct((m, A_WIDTH), BF16),
                   jax.ShapeDtypeStruct((m, A_WIDTH), F32)),
        compiler_params=_params(None),
        name="spatial_tail",
    )(ug, gv, g_anorm.reshape(1, A_WIDTH), w_s, b_s)


def _log_gamma(shape, head):
    return jnp.log1p(-jnp.exp2(jnp.full(shape, -5.0, F32) - head.astype(F32)))


def _group_norm(o, g):
    mu = jnp.mean(o, axis=-1, keepdims=True)
    var = jnp.mean(jnp.square(o - mu), axis=-1, keepdims=True)
    return (o - mu) * lax.rsqrt(var + EPS) * g


def _ret_main_kernel(*refs, side_fn, n_side_in):
    q_ref, kt_ref, v_ref, sg_ref, gr_ref = refs[:5]
    side_in = refs[5:5 + n_side_in]
    o_ref, r_ref = refs[5 + n_side_in:7 + n_side_in]
    side_out = refs[7 + n_side_in:]
    head = pl.program_id(1)
    L, dk = CHUNK, RET_HEAD_DIM
    ri = lax.broadcasted_iota(jnp.int32, (L, L), 0).astype(F32)
    ci = lax.broadcasted_iota(jnp.int32, (L, L), 1).astype(F32)
    diff = ri - ci
    decay_in = jnp.where(diff >= 0, jnp.exp(_log_gamma((L, L), head) * jnp.maximum(diff, 0.0)), 0.0)
    rw = lax.broadcasted_iota(jnp.int32, (L, dk), 0).astype(F32)
    decay_q = jnp.exp(_log_gamma((L, dk), head) * (rw + 1.0))
    cw = lax.broadcasted_iota(jnp.int32, (dk, L), 1).astype(F32)
    decay_kt = jnp.exp(_log_gamma((dk, L), head) * (L - 1.0 - cw))
    decay_blk = jnp.exp(_log_gamma((dk, dk), head) * L)
    gr = gr_ref[...]

    R = jnp.zeros((dk, dk), F32)
    for i in range(SEQ // L):
        rs = slice(i * L, (i + 1) * L)
        qi, kti, vi = q_ref[rs, :], kt_ref[:, rs], v_ref[rs, :]
        s = jnp.dot(qi, kti, preferred_element_type=F32) * decay_in
        o = (jnp.dot(s.astype(BF16), vi, preferred_element_type=F32)
             + jnp.dot(qi, R.astype(BF16), preferred_element_type=F32) * decay_q)
        kdt = (kti.astype(F32) * decay_kt).astype(BF16)
        R = R * decay_blk + jnp.dot(kdt, vi, preferred_element_type=F32)
        o_ref[rs, :] = (_group_norm(o, gr) * sg_ref[rs, :]).astype(o_ref.dtype)
    r_ref[...] = R
    side_fn(pl.program_id(0) * RET_HEADS + head, *side_in, *side_out)


def _ret_main(q, kt, v_src, sg_src, g_ret, side):
    dk = RET_HEAD_DIM
    (v, v_blk), (sg, sg_blk) = v_src, sg_src
    blk = lambda off: pl.BlockSpec((SEQ, dk), lambda b, h: (b, off + h))
    step = lambda b, h: b * RET_HEADS + h
    assert BATCH * RET_HEADS == side["n_blocks"]
    in_specs = [blk(0), pl.BlockSpec((dk, SEQ), lambda b, h: (h, b)), blk(v_blk), blk(sg_blk),
                pl.BlockSpec((1, dk), lambda b, h: (0, h))]
    in_specs += [pl.BlockSpec(shape, lambda b, h, f=f: f(step(b, h))) for shape, f in side["in_specs"]]
    out_specs = [blk(0), pl.BlockSpec((None, None, None, dk, dk), lambda b, h: (0, b, h, 0, 0))]
    out_specs += [pl.BlockSpec(shape, lambda b, h, f=f: f(step(b, h))) for shape, f in side["out_specs"]]
    out_shape = [jax.ShapeDtypeStruct((M_MAIN, RET_WIDTH), BF16),
                 jax.ShapeDtypeStruct((DEPTH, BATCH, RET_HEADS, dk, dk), F32)] + side["out_shape"]
    kern = functools.partial(_ret_main_kernel, side_fn=side["fn"], n_side_in=len(side["args"]))
    return pl.pallas_call(
        kern, grid=(BATCH, RET_HEADS), in_specs=in_specs, out_specs=out_specs, out_shape=out_shape,
        compiler_params=_params(("arbitrary", "arbitrary")), name="ret_main",
    )(q, kt, v, sg, g_ret.reshape(1, RET_WIDTH), *side["args"])


RET_TAIL_ROWS = 16


def _ret_tail_block(blk, q_ref, k_ref, v_ref, sg_ref, gr_ref, s_ref, o_ref, so_ref):
    head = blk % RET_HEADS
    nb, dk = q_ref.shape
    gam_row = jnp.exp(_log_gamma((nb, dk), head))
    gam_st = jnp.exp(_log_gamma((dk, dk), head))
    q, k, v = q_ref[...], k_ref[...], v_ref[...]
    qt, kt = q.T, k.T
    rows = []
    for r in range(nb):
        R = s_ref[r]
        so_ref[r] = R * gam_st + kt[:, r:r + 1] * v[r:r + 1, :]
        rows.append(jnp.sum(qt[:, r:r + 1] * R, axis=0, keepdims=True))
    qr = jnp.concatenate(rows, axis=0)
    qk = jnp.sum(q * k, axis=-1, keepdims=True)
    o = qk * v + qr * gam_row
    o_ref[...] = _group_norm(o, gr_ref[...]) * sg_ref[...]


def _ret_tail_side(q, k, v_src, sg_src, g_ret, state):
    dk, nb = RET_HEAD_DIM, RET_TAIL_ROWS
    (v, v_blk), (sg, sg_blk) = v_src, sg_src
    row = lambda off: ((nb, dk), lambda i: (i // RET_HEADS, off + i % RET_HEADS))
    st = ((None, nb, None, dk, dk), lambda i: (0, i // RET_HEADS, i % RET_HEADS, 0, 0))
    return _side(
        _ret_tail_block, (DEC_BATCH // nb) * RET_HEADS,
        args=[q, k, v, sg, g_ret.reshape(1, RET_WIDTH), state],
        in_specs=[row(0), row(0), row(v_blk), row(sg_blk),
                  ((1, dk), lambda i: (0, i % RET_HEADS)), st],
        out_specs=[row(0), st],
        out_shape=[jax.ShapeDtypeStruct((M_TAIL, RET_WIDTH), F32),
                   jax.ShapeDtypeStruct(state.shape, F32)])


XATTN_ROWS = 512


def _xattn_main_block(q_ref, k_ref, v_ref, g_ref, o_ref):
    k = k_ref[...].astype(BF16)
    v = v_ref[...].astype(BF16)
    for t in range(q_ref.shape[0] // XATTN_ROWS):
        rs = slice(t * XATTN_ROWS, (t + 1) * XATTN_ROWS)
        sc = lax.dot_general(q_ref[rs, :], k, (((1,), (1,)), ((), ())), preferred_element_type=F32)
        sc = sc * (MEM_HEAD_DIM ** -0.5)
        e = jnp.exp(sc - jnp.max(sc, axis=-1, keepdims=True))
        p = e / jnp.sum(e, axis=-1, keepdims=True)
        om = jnp.dot(p.astype(BF16), v, preferred_element_type=F32)
        o_ref[rs, :] = (om * g_ref[rs, :]).astype(o_ref.dtype)


def _xattn_main_side(cq_src, scg_src, kv):
    dh = MEM_HEAD_DIM
    (cq, cq_blk), (scg, scg_blk) = cq_src, scg_src
    qspec = lambda off: ((SEQ, dh), lambda i: (i // MEM_HEADS, off + i % MEM_HEADS))
    kvspec = lambda off: ((MEM_LEN, dh), lambda i: (i // MEM_HEADS, off + i % MEM_HEADS))
    return _side(
        lambda blk, *refs: _xattn_main_block(*refs), BATCH * MEM_HEADS,
        args=[cq, kv, kv, scg],
        in_specs=[qspec(cq_blk), kvspec(0), kvspec(MEM_HEADS), qspec(scg_blk)],
        out_specs=[qspec(0)],
        out_shape=[jax.ShapeDtypeStruct((M_MAIN, MEM_WIDTH), BF16)])


XATTN_TAIL_ROWS = 4


def _xattn_tail_block(q_ref, g_ref, k_ref, v_ref, o_ref):
    for r in range(q_ref.shape[0]):
        q = q_ref[r]
        sc = jnp.sum(k_ref[r] * q[None], axis=-1, keepdims=True) * (MEM_HEAD_DIM ** -0.5)
        e = jnp.exp(sc - jnp.max(sc, axis=0, keepdims=True))
        o_ref[r] = jnp.sum(e * v_ref[r], axis=0) / jnp.sum(e, axis=0) * g_ref[r]


def _xattn_tail_side(cq, scg, ck, cv):
    nb = XATTN_TAIL_ROWS
    shape3 = (M_TAIL, MEM_HEADS, MEM_HEAD_DIM)
    row = ((nb, MEM_HEADS, MEM_HEAD_DIM), lambda i: (i, 0, 0))
    kv = ((None, nb, MEM_LEN, MEM_HEADS, MEM_HEAD_DIM), lambda i: (0, i, 0, 0, 0))
    return _side(
        lambda blk, *refs: _xattn_tail_block(*refs), DEC_BATCH // nb,
        args=[cq.reshape(shape3), scg.reshape(shape3), ck, cv],
        in_specs=[row, row, kv, kv], out_specs=[row],
        out_shape=[jax.ShapeDtypeStruct(shape3, F32)])


def kernel(x_prompt, x_sample, state_ret, cache_mem_k, cache_mem_v, mem_prompt, g_pre, w_in, g_anorm,
           w_s, b_s, g_ret, g_mem, w_mem_kv, w_out_a, w_out_b, w_out_c, w_out, g_final):
    assert DEPTH == 1 and w_in.shape == (DEPTH, D_MODEL, IN_WIDTH)
    xp = x_prompt.reshape(M_MAIN, D_MODEL)
    xs = x_sample.reshape(M_TAIL, D_MODEL)
    win = w_in.reshape(D_MODEL, IN_WIDTH)

    h_m = _rmsnorm(xp, g_pre[0], 512, BF16)
    h_t = _rmsnorm(xs, g_pre[0], M_TAIL, BF16)
    cos_m, sin_m, cos_t, sin_t = _rope_tables()

    hm = _rmsnorm(mem_prompt.reshape(BATCH * MEM_LEN, D_MODEL), g_mem[0], 512, BF16)
    wkv = w_mem_kv.reshape(D_MODEL, 2 * MEM_WIDTH)
    (kv,) = _wres_matmul([(hm, None, wkv, 0)], [], [(F32, None)], n_cols=2 * MEM_WIDTH, tn=1024,
                         tm=512, sub=256, epilogue=_epi_identity, name="mem_kv")

    tn = tm = 1024

    def inproj(cols, n_cols, epilogue, out_dtypes, name, tn=tn, exts=(), **kw):
        return _wres_matmul([(h_m, h_t, win, c) for c in cols], list(exts), [out_dtypes],
                            n_cols=n_cols, tn=tn, tm=tm, sub=256, epilogue=epilogue, name=name, **kw)

    ring = dict(lhs_ring=True, vmem_bytes=VMEM_HOST_BYTES)
    second = RET_WIDTH // tn
    id_m, id_t = inproj([((0, OFF_RV), (second, OFF_CQ))], RET_WIDTH + MEM_WIDTH, _epi_identity,
                        (BF16, F32), "inproj_v_cq", **ring)
    silu_m, silu_t = inproj([((0, OFF_RG), (second, OFF_CG))], RET_WIDTH + MEM_WIDTH, _epi_silu,
                            (BF16, F32), "inproj_rg_cg", **ring)
    gv_m, gv_t = inproj([OFF_AV], A_WIDTH, _epi_gelu, (BF16, F32), "inproj_gv", **ring)
    ug_m, ug_t = inproj([OFF_AU, OFF_AG], A_WIDTH, _epi_gelu_silu, (BF16, F32), "inproj_ug", tn=512,
                        **ring)

    rope = [(cos_m, cos_t, "rope", 0), (sin_m, sin_t, "rope", 0)]
    q_m, q_t, a_m = inproj(
        [OFF_RQ], RET_WIDTH, functools.partial(_epi_rope, 1.0), (BF16, F32), "inproj_q", exts=rope,
        tn=512, side=_spatial_main_side(ug_m, (gv_m, 0), g_anorm[0], w_s[0], b_s[0]), **ring)
    a_t, vn_t = _spatial_tail(ug_t, gv_t, 0, g_anorm[0], w_s[0], b_s[0])
    mem_blk = RET_WIDTH // MEM_HEAD_DIM
    kt_m, k_t, c_m = inproj(
        [OFF_RK], RET_WIDTH, functools.partial(_epi_rope, RET_HEAD_DIM ** -0.5), (BF16, F32),
        "inproj_k", exts=rope, main_transposed=True,
        side=_xattn_main_side((id_m, mem_blk), (silu_m, mem_blk), kv),
        vmem_bytes=VMEM_HOST_BYTES)
    gs_m, gs_t, b_t, ret_s = inproj(
        [OFF_GATES], N_BRANCH * D_MODEL, _epi_sigmoid, (BF16, BF16), "inproj_gates",
        side=_ret_tail_side(q_t, k_t, (id_t, 0), (silu_t, 0), g_ret[0], state_ret),
        vmem_bytes=VMEM_HOST_BYTES)

    b_m, ret_p, c_t3 = _ret_main(
        q_m, kt_m, (id_m, 0), (silu_m, 0), g_ret[0],
        side=_xattn_tail_side(id_t[:, RET_WIDTH:], silu_t[:, RET_WIDTH:], cache_mem_k, cache_mem_v))
    c_t = c_t3.reshape(M_TAIL, MEM_WIDTH)

    merged_m, merged_t = _wres_matmul(
        [(a_m, a_t, w_out_a.reshape(A_WIDTH, D_MODEL), 0),
         (b_m, b_t, w_out_b.reshape(RET_WIDTH, D_MODEL), 0),
         (c_m, c_t, w_out_c.reshape(MEM_WIDTH, D_MODEL), 0)],
        [(gs_m, gs_t, "tile", b * D_MODEL) for b in range(N_BRANCH)],
        [(BF16, BF16)], n_cols=D_MODEL, tn=512, tm=1024, sub=256, epilogue=_epi_merge, name="merge",
        **ring)
    y_m, y_t = _wres_matmul(
        [(merged_m, merged_t, w_out.reshape(D_MODEL, D_MODEL), 0)],
        [(xp, xs, "tile", 0)], [(F32, F32)], n_cols=D_MODEL, tn=1024, tm=1024, sub=256,
        epilogue=_epi_residual, name="outproj", vmem_bytes=VMEM_HOST_BYTES)
    y_prompt = _rmsnorm(y_m, g_final, 512, F32).reshape(BATCH, SEQ, D_MODEL)
    y_sample = _rmsnorm(y_t, g_final, M_TAIL, F32).reshape(DEC_BATCH, DEC_SEQ, D_MODEL)

    mem_shape = (DEPTH, BATCH, MEM_LEN, MEM_HEADS, MEM_HEAD_DIM)
    return (y_prompt, y_sample, ret_p,
            kv[:, :MEM_WIDTH].reshape(mem_shape), kv[:, MEM_WIDTH:].reshape(mem_shape),
            ret_s,
            vn_t.reshape(DEPTH, DEC_BATCH, DEC_SEQ, A_WIDTH))
```

```python
import functools

import jax
import jax.numpy as jnp
from jax import lax
from jax.experimental import pallas as pl
from jax.experimental.pallas import tpu as pltpu

D_MODEL = 4096
BATCH = 4
SEQ = 2048
DEPTH = 1
DEC_BATCH = 128
DEC_SEQ = 1
PAST_LEN = 16384

CHUNK = 128
A_WIDTH = 2048
A_GROUPS = 4
RET_HEADS = 8
RET_HEAD_DIM = 256
RET_WIDTH = RET_HEADS * RET_HEAD_DIM
MEM_LEN = 256
MEM_HEADS = 4
MEM_HEAD_DIM = 256
MEM_WIDTH = MEM_HEADS * MEM_HEAD_DIM
N_BRANCH = 3
ROPE_BASE = 10000.0
EPS = 1e-6

OFF_AU = 0
OFF_AV = OFF_AU + A_WIDTH
OFF_AG = OFF_AV + A_WIDTH
OFF_RQ = OFF_AG + A_WIDTH
OFF_RK = OFF_RQ + RET_WIDTH
OFF_RV = OFF_RK + RET_WIDTH
OFF_RG = OFF_RV + RET_WIDTH
OFF_CQ = OFF_RG + RET_WIDTH
OFF_CG = OFF_CQ + MEM_WIDTH
OFF_GATES = OFF_CG + MEM_WIDTH
IN_WIDTH = OFF_GATES + N_BRANCH * D_MODEL

M_MAIN = BATCH * SEQ
M_TAIL = DEC_BATCH * DEC_SEQ
ROPE_HALF = RET_HEAD_DIM // 2

F32 = jnp.float32
BF16 = jnp.bfloat16

MIB = 1024 * 1024
VMEM_BUDGET_BYTES = 56 * MIB
VMEM_HOST_BYTES = 63 * MIB


def _params(semantics, vmem_bytes=VMEM_BUDGET_BYTES):
    return pltpu.CompilerParams(dimension_semantics=semantics, vmem_limit_bytes=vmem_bytes)


def _rmsnorm_kernel(x_ref, g_ref, o_ref):
    x = x_ref[...].astype(F32)
    y = x * lax.rsqrt(jnp.mean(x * x, axis=-1, keepdims=True) + EPS)
    o_ref[...] = (y * g_ref[...].astype(F32)).astype(o_ref.dtype)


def _rmsnorm(x, g, rows, out_dtype):
    m, d = x.shape
    return pl.pallas_call(
        _rmsnorm_kernel,
        grid=(m // rows,),
        in_specs=[pl.BlockSpec((rows, d), lambda i: (i, 0)),
                  pl.BlockSpec((1, d), lambda i: (0, 0))],
        out_specs=pl.BlockSpec((rows, d), lambda i: (i, 0)),
        out_shape=jax.ShapeDtypeStruct((m, d), out_dtype),
        compiler_params=_params(("arbitrary",)),
        name="rmsnorm",
    )(x, g.reshape(1, d))


def _rope_kernel(cm_ref, sm_ref, ct_ref, st_ref):
    def table(rows, pos):
        j = lax.broadcasted_iota(jnp.int32, (rows, ROPE_HALF), 1).astype(F32)
        inv = ROPE_BASE ** (-j / ROPE_HALF)
        return pos.astype(F32) * inv

    ang = table(SEQ, lax.broadcasted_iota(jnp.int32, (SEQ, ROPE_HALF), 0))
    cm_ref[...] = jnp.cos(ang)
    sm_ref[...] = jnp.sin(ang)
    r = lax.broadcasted_iota(jnp.int32, (M_TAIL, ROPE_HALF), 0)
    t = jnp.zeros_like(r) if DEC_SEQ == 1 else lax.rem(r, DEC_SEQ)
    ang_t = table(M_TAIL, PAST_LEN + t)
    ct_ref[...] = jnp.cos(ang_t)
    st_ref[...] = jnp.sin(ang_t)


def _rope_tables():
    return pl.pallas_call(
        _rope_kernel,
        out_shape=(jax.ShapeDtypeStruct((SEQ, ROPE_HALF), F32),
                   jax.ShapeDtypeStruct((SEQ, ROPE_HALF), F32),
                   jax.ShapeDtypeStruct((M_TAIL, ROPE_HALF), F32),
                   jax.ShapeDtypeStruct((M_TAIL, ROPE_HALF), F32)),
        name="rope_tables",
    )()


CAST_ROWS = 256


def _wres_kernel(*refs, lhs_of, w_cols, n_tiles, n_ext, n_out, mt, has_tail, sub, epilogue,
                 side_fn, n_side_in, n_side_out, main_transposed):
    refs = list(refs)
    n_pairs, n_lhs = len(lhs_of), max(lhs_of) + 1

    def take(k):
        out = refs[:k]
        del refs[:k]
        return out

    lhs_main = take(n_lhs)
    lhs_tail = take(n_lhs) if has_tail else []
    w_hbm = take(n_pairs)
    ext_main = take(n_ext)
    ext_tail = take(n_ext) if has_tail else []
    side_in = take(n_side_in)
    out_main = take(n_out)
    out_tail = take(n_out) if has_tail else []
    side_out = take(n_side_out)
    wf = take(n_pairs)
    wb = take(n_pairs)
    (sem,) = take(1)

    n = pl.program_id(0)
    m = pl.program_id(1)
    tn = wb[0].shape[1]

    def w_copy(p, tile):
        col = w_cols[p][0][1] + tile * tn
        for first_tile, first_col in w_cols[p][1:]:
            col = jnp.where(tile >= first_tile, first_col + (tile - first_tile) * tn, col)
        col = pl.multiple_of(col, tn)
        return pltpu.make_async_copy(w_hbm[p].at[:, pl.ds(col, tn)], wf[p], sem.at[p])

    @pl.when(m == 0)
    def _weights():
        @pl.when(n == 0)
        def _first():
            for p in range(n_pairs):
                w_copy(p, 0).start()

        for p in range(n_pairs):
            w_copy(p, n).wait()

            def body(i, carry, wf_ref=wf[p], wb_ref=wb[p]):
                r = pl.multiple_of(i * CAST_ROWS, CAST_ROWS)
                wb_ref[pl.ds(r, CAST_ROWS), :] = wf_ref[pl.ds(r, CAST_ROWS), :].astype(BF16)
                return carry
            lax.fori_loop(0, wf[p].shape[0] // CAST_ROWS, body, 0)

        @pl.when(n + 1 < n_tiles)
        def _next():
            for p in range(n_pairs):
                w_copy(p, n + 1).start()

    def body(lhs, ext, outs, transposed=False):
        for c in range(0, tn, sub):
            accs = [jnp.dot(lhs[i][...].astype(BF16), b[:, c:c + sub], preferred_element_type=F32)
                    for i, b in zip(lhs_of, wb)]
            res = epilogue(accs, ext, c, sub, n)
            for o, r in zip(outs, res):
                if transposed:
                    o[c:c + sub, :] = r.T.astype(o.dtype)
                else:
                    o[:, c:c + sub] = r.astype(o.dtype)

    body(lhs_main, ext_main, out_main, main_transposed)
    if side_fn is not None:
        side_fn(n, m, *side_in, *side_out)
    if has_tail:
        @pl.when(m == mt - 1)
        def _tail():
            body(lhs_tail, ext_tail, out_tail)


def _wres_matmul(pairs, exts, outs, *, n_cols, tn, tm, sub, epilogue, name, side=None,
                 vmem_bytes=VMEM_BUDGET_BYTES, main_transposed=False):
    m_main = pairs[0][0].shape[0]
    has_tail = pairs[0][1] is not None
    mt = m_main // tm
    n_tiles = n_cols // tn
    grid = (n_tiles, mt)

    lhs, lhs_of = [], []
    for lm, lt, _, _ in pairs:
        ids = [i for i, (a, _) in enumerate(lhs) if a is lm]
        if not ids:
            lhs.append((lm, lt))
        lhs_of.append(ids[0] if ids else len(lhs) - 1)

    in_specs, args = [], []
    for lm, _ in lhs:
        in_specs.append(pl.BlockSpec((tm, lm.shape[1]), lambda n, m: (m, 0)))
        args.append(lm)
    if has_tail:
        for _, lt in lhs:
            in_specs.append(pl.BlockSpec(lt.shape, lambda n, m: (0, 0)))
            args.append(lt)
    w_cols = []
    for _, _, w, cols in pairs:
        segs = ((0, cols),) if isinstance(cols, int) else tuple(cols)
        for i, (first_tile, first_col) in enumerate(segs):
            last_tile = segs[i + 1][0] if i + 1 < len(segs) else n_tiles
            assert first_col % tn == 0 and first_col + (last_tile - first_tile) * tn <= w.shape[1]
        w_cols.append(segs)
        in_specs.append(pl.BlockSpec(memory_space=pl.ANY))
        args.append(w)

    def ext_spec(arr, kind, off, tail):
        rows = arr.shape[0] if tail else tm
        if kind == "tile":
            assert off % tn == 0
            if tail:
                return pl.BlockSpec((rows, tn), lambda n, m, o=off // tn: (0, o + n))
            return pl.BlockSpec((rows, tn), lambda n, m, o=off // tn: (m, o + n))
        assert kind == "rope"
        if tail:
            return pl.BlockSpec(arr.shape, lambda n, m: (0, 0))
        per = arr.shape[0] // tm
        return pl.BlockSpec((tm, arr.shape[1]), lambda n, m: (lax.rem(m, per), 0))

    for em, _, kind, off in exts:
        in_specs.append(ext_spec(em, kind, off, False))
        args.append(em)
    if has_tail:
        for _, et, kind, off in exts:
            in_specs.append(ext_spec(et, kind, off, True))
            args.append(et)

    out_specs, out_shape = [], []
    for dm, _ in outs:
        if main_transposed:
            out_specs.append(pl.BlockSpec((tn, tm), lambda n, m: (n, m)))
            out_shape.append(jax.ShapeDtypeStruct((n_cols, m_main), dm))
        else:
            out_specs.append(pl.BlockSpec((tm, tn), lambda n, m: (m, n)))
            out_shape.append(jax.ShapeDtypeStruct((m_main, n_cols), dm))
    if has_tail:
        m_tail = pairs[0][1].shape[0]
        for _, dt in outs:
            out_specs.append(pl.BlockSpec((m_tail, tn), lambda n, m: (0, n)))
            out_shape.append(jax.ShapeDtypeStruct((m_tail, n_cols), dt))

    if side is not None:
        assert n_tiles * mt >= side["n_blocks"]
        step = lambda n, m: jnp.minimum(n * mt + m, side["n_blocks"] - 1)
        in_specs += [pl.BlockSpec(shape, lambda n, m, f=f: f(step(n, m)))
                     for shape, f in side["in_specs"]]
        args += side["args"]
        out_specs += [pl.BlockSpec(shape, lambda n, m, f=f: f(step(n, m)))
                      for shape, f in side["out_specs"]]
        out_shape += side["out_shape"]
        side_fn = lambda n, m, *refs: side["fn"](step(n, m), *refs)
    else:
        side_fn = None

    scratch = ([pltpu.VMEM((w.shape[0], tn), F32) for _, _, w, _ in pairs]
               + [pltpu.VMEM((w.shape[0], tn), BF16) for _, _, w, _ in pairs]
               + [pltpu.SemaphoreType.DMA((len(pairs),))])
    kern = functools.partial(_wres_kernel, lhs_of=tuple(lhs_of), w_cols=tuple(w_cols),
                             n_tiles=n_tiles, n_ext=len(exts), n_out=len(outs),
                             mt=mt, has_tail=has_tail, sub=sub, epilogue=epilogue,
                             side_fn=side_fn,
                             n_side_in=len(side["args"]) if side else 0,
                             n_side_out=len(side["out_shape"]) if side else 0,
                             main_transposed=main_transposed)
    return pl.pallas_call(
        kern, grid=grid, in_specs=in_specs, out_specs=out_specs, out_shape=out_shape,
        scratch_shapes=scratch, compiler_params=_params(("arbitrary", "arbitrary"), vmem_bytes),
        name=name,
    )(*args)


def _side(fn, n_blocks, args, in_specs, out_specs, out_shape):
    return dict(fn=fn, n_blocks=n_blocks, args=args, in_specs=in_specs, out_specs=out_specs,
                out_shape=out_shape)


def _sigmoid(x):
    return 0.5 * jnp.tanh(0.5 * x) + 0.5


def _epi_identity(accs, ext, c, sub, n):
    return [accs[0]]


def _epi_sigmoid(accs, ext, c, sub, n):
    return [_sigmoid(accs[0])]


def _epi_gelu_silu(accs, ext, c, sub, n):
    return [jax.nn.gelu(accs[0]) * (accs[1] * _sigmoid(accs[1]))]


def _epi_gelu(accs, ext, c, sub, n):
    return [jax.nn.gelu(accs[0])]


def _epi_silu(accs, ext, c, sub, n):
    return [accs[0] * _sigmoid(accs[0])]


def _epi_rope(scale, accs, ext, c, sub, n):
    assert sub == RET_HEAD_DIM
    x = accs[0]
    x1, x2 = x[:, :ROPE_HALF], x[:, ROPE_HALF:]
    cos, sin = ext[0][...], ext[1][...]
    out = jnp.concatenate([x1 * cos - x2 * sin, x1 * sin + x2 * cos], axis=-1)
    return [out if scale == 1.0 else out * scale]


def _epi_merge(accs, ext, c, sub, n):
    g0, g1, g2 = (e[:, c:c + sub] for e in ext)
    return [g0 * accs[0] + g1 * accs[1] + g2 * accs[2]]


def _epi_residual(accs, ext, c, sub, n):
    return [ext[0][:, c:c + sub] + accs[0]]


SPATIAL_ROWS = 256


def _layer_norm(x, g):
    mu = jnp.mean(x, axis=-1, keepdims=True)
    var = jnp.mean(jnp.square(x - mu), axis=-1, keepdims=True)
    return (x - mu) * lax.rsqrt(var + EPS) * g


def _spatial_main_block(ug_ref, gv_ref, ga_ref, ws_ref, bst_ref, o_ref):
    rows = gv_ref.shape[0]
    gw = A_WIDTH // A_GROUPS
    vn = _layer_norm(gv_ref[...].astype(F32), ga_ref[...]).astype(BF16)
    ri = lax.broadcasted_iota(jnp.int32, (CHUNK, CHUNK), 0)
    ci = lax.broadcasted_iota(jnp.int32, (CHUNK, CHUNK), 1)
    for g in range(A_GROUPS):
        w = jnp.where(ri >= ci, ws_ref[g], 0.0).astype(BF16)
        bias = bst_ref[:, g:g + 1]
        for c in range(rows // CHUNK):
            rs = slice(c * CHUNK, (c + 1) * CHUNK)
            cs = slice(g * gw, (g + 1) * gw)
            sp = jnp.dot(w, vn[rs, cs], preferred_element_type=F32) + bias
            o_ref[rs, cs] = (ug_ref[rs, cs] * sp).astype(o_ref.dtype)


def _spatial_main_side(ug, gv_src, g_anorm, w_s, b_s):
    gv, gv_blk = gv_src
    rows = SPATIAL_ROWS
    return _side(
        lambda blk, *refs: _spatial_main_block(*refs), M_MAIN // rows,
        args=[ug, gv, g_anorm.reshape(1, A_WIDTH), w_s, b_s.T],
        in_specs=[((rows, A_WIDTH), lambda i: (i, 0)),
                  ((rows, A_WIDTH), lambda i: (i, gv_blk)),
                  ((1, A_WIDTH), lambda i: (0, 0)),
                  ((A_GROUPS, CHUNK, CHUNK), lambda i: (0, 0, 0)),
                  ((CHUNK, A_GROUPS), lambda i: (0, 0))],
        out_specs=[((rows, A_WIDTH), lambda i: (i, 0))],
        out_shape=[jax.ShapeDtypeStruct((M_MAIN, A_WIDTH), BF16)])


def _spatial_tail_kernel(ug_ref, gv_ref, ga_ref, ws_ref, bs_ref, o_ref, vn_ref, *, gv_col):
    gw = A_WIDTH // A_GROUPS
    vn = _layer_norm(gv_ref[:, gv_col:gv_col + A_WIDTH], ga_ref[...])
    vn_ref[...] = vn
    for g in range(A_GROUPS):
        cs = slice(g * gw, (g + 1) * gw)
        sp = vn[:, cs] * ws_ref[g, 0:1, 0:1] + bs_ref[g:g + 1, 0:1]
        o_ref[:, cs] = (ug_ref[:, cs] * sp).astype(o_ref.dtype)


def _spatial_tail(ug, gv, gv_col, g_anorm, w_s, b_s):
    m = ug.shape[0]
    return pl.pallas_call(
        functools.partial(_spatial_tail_kernel, gv_col=gv_col),
        out_shape=(jax.ShapeDtypeStruct((m, A_WIDTH), BF16),
                   jax.ShapeDtypeStruct((m, A_WIDTH), F32)),
        compiler_params=_params(None),
        name="spatial_tail",
    )(ug, gv, g_anorm.reshape(1, A_WIDTH), w_s, b_s)


def _log_gamma(shape, head):
    return jnp.log1p(-jnp.exp2(jnp.full(shape, -5.0, F32) - head.astype(F32)))


def _group_norm(o, g):
    mu = jnp.mean(o, axis=-1, keepdims=True)
    var = jnp.mean(jnp.square(o - mu), axis=-1, keepdims=True)
    return (o - mu) * lax.rsqrt(var + EPS) * g


def _ret_main_kernel(*refs, side_fn, n_side_in):
    q_ref, kt_ref, v_ref, sg_ref, gr_ref = refs[:5]
    side_in = refs[5:5 + n_side_in]
    o_ref, r_ref = refs[5 + n_side_in:7 + n_side_in]
    side_out = refs[7 + n_side_in:]
    head = pl.program_id(1)
    L, dk = CHUNK, RET_HEAD_DIM
    ri = lax.broadcasted_iota(jnp.int32, (L, L), 0).astype(F32)
    ci = lax.broadcasted_iota(jnp.int32, (L, L), 1).astype(F32)
    diff = ri - ci
    decay_in = jnp.where(diff >= 0, jnp.exp(_log_gamma((L, L), head) * jnp.maximum(diff, 0.0)), 0.0)
    rw = lax.broadcasted_iota(jnp.int32, (L, dk), 0).astype(F32)
    decay_q = jnp.exp(_log_gamma((L, dk), head) * (rw + 1.0))
    cw = lax.broadcasted_iota(jnp.int32, (dk, L), 1).astype(F32)
    decay_kt = jnp.exp(_log_gamma((dk, L), head) * (L - 1.0 - cw))
    decay_blk = jnp.exp(_log_gamma((dk, dk), head) * L)
    gr = gr_ref[...]

    R = jnp.zeros((dk, dk), F32)
    for i in range(SEQ // L):
        rs = slice(i * L, (i + 1) * L)
        qi, kti, vi = q_ref[rs, :], kt_ref[:, rs], v_ref[rs, :]
        s = jnp.dot(qi, kti, preferred_element_type=F32) * decay_in
        o = (jnp.dot(s.astype(BF16), vi, preferred_element_type=F32)
             + jnp.dot(qi, R.astype(BF16), preferred_element_type=F32) * decay_q)
        kdt = (kti.astype(F32) * decay_kt).astype(BF16)
        R = R * decay_blk + jnp.dot(kdt, vi, preferred_element_type=F32)
        o_ref[rs, :] = (_group_norm(o, gr) * sg_ref[rs, :]).astype(o_ref.dtype)
    r_ref[...] = R
    side_fn(pl.program_id(0) * RET_HEADS + head, *side_in, *side_out)


def _ret_main(q, kt, v_src, sg_src, g_ret, side):
    dk = RET_HEAD_DIM
    (v, v_blk), (sg, sg_blk) = v_src, sg_src
    blk = lambda off: pl.BlockSpec((SEQ, dk), lambda b, h: (b, off + h))
    step = lambda b, h: b * RET_HEADS + h
    assert BATCH * RET_HEADS == side["n_blocks"]
    in_specs = [blk(0), pl.BlockSpec((dk, SEQ), lambda b, h: (h, b)), blk(v_blk), blk(sg_blk),
                pl.BlockSpec((1, dk), lambda b, h: (0, h))]
    in_specs += [pl.BlockSpec(shape, lambda b, h, f=f: f(step(b, h))) for shape, f in side["in_specs"]]
    out_specs = [blk(0), pl.BlockSpec((None, None, None, dk, dk), lambda b, h: (0, b, h, 0, 0))]
    out_specs += [pl.BlockSpec(shape, lambda b, h, f=f: f(step(b, h))) for shape, f in side["out_specs"]]
    out_shape = [jax.ShapeDtypeStruct((M_MAIN, RET_WIDTH), BF16),
                 jax.ShapeDtypeStruct((DEPTH, BATCH, RET_HEADS, dk, dk), F32)] + side["out_shape"]
    kern = functools.partial(_ret_main_kernel, side_fn=side["fn"], n_side_in=len(side["args"]))
    return pl.pallas_call(
        kern, grid=(BATCH, RET_HEADS), in_specs=in_specs, out_specs=out_specs, out_shape=out_shape,
        compiler_params=_params(("arbitrary", "arbitrary")), name="ret_main",
    )(q, kt, v, sg, g_ret.reshape(1, RET_WIDTH), *side["args"])


RET_TAIL_ROWS = 16


def _ret_tail_block(blk, q_ref, k_ref, v_ref, sg_ref, gr_ref, s_ref, o_ref, so_ref):
    head = blk % RET_HEADS
    nb, dk = q_ref.shape
    gam_row = jnp.exp(_log_gamma((nb, dk), head))
    gam_st = jnp.exp(_log_gamma((dk, dk), head))
    q, k, v = q_ref[...], k_ref[...], v_ref[...]
    qt, kt = q.T, k.T
    rows = []
    for r in range(nb):
        R = s_ref[r]
        so_ref[r] = R * gam_st + kt[:, r:r + 1] * v[r:r + 1, :]
        rows.append(jnp.sum(qt[:, r:r + 1] * R, axis=0, keepdims=True))
    qr = jnp.concatenate(rows, axis=0)
    qk = jnp.sum(q * k, axis=-1, keepdims=True)
    o = qk * v + qr * gam_row
    o_ref[...] = _group_norm(o, gr_ref[...]) * sg_ref[...]


def _ret_tail_side(q, k, v_src, sg_src, g_ret, state):
    dk, nb = RET_HEAD_DIM, RET_TAIL_ROWS
    (v, v_blk), (sg, sg_blk) = v_src, sg_src
    row = lambda off: ((nb, dk), lambda i: (i // RET_HEADS, off + i % RET_HEADS))
    st = ((None, nb, None, dk, dk), lambda i: (0, i // RET_HEADS, i % RET_HEADS, 0, 0))
    return _side(
        _ret_tail_block, (DEC_BATCH // nb) * RET_HEADS,
        args=[q, k, v, sg, g_ret.reshape(1, RET_WIDTH), state],
        in_specs=[row(0), row(0), row(v_blk), row(sg_blk),
                  ((1, dk), lambda i: (0, i % RET_HEADS)), st],
        out_specs=[row(0), st],
        out_shape=[jax.ShapeDtypeStruct((M_TAIL, RET_WIDTH), F32),
                   jax.ShapeDtypeStruct(state.shape, F32)])


XATTN_ROWS = 512


def _xattn_main_block(q_ref, k_ref, v_ref, g_ref, o_ref):
    k = k_ref[...].astype(BF16)
    v = v_ref[...].astype(BF16)
    for t in range(q_ref.shape[0] // XATTN_ROWS):
        rs = slice(t * XATTN_ROWS, (t + 1) * XATTN_ROWS)
        sc = lax.dot_general(q_ref[rs, :], k, (((1,), (1,)), ((), ())), preferred_element_type=F32)
        sc = sc * (MEM_HEAD_DIM ** -0.5)
        e = jnp.exp(sc - jnp.max(sc, axis=-1, keepdims=True))
        p = e / jnp.sum(e, axis=-1, keepdims=True)
        om = jnp.dot(p.astype(BF16), v, preferred_element_type=F32)
        o_ref[rs, :] = (om * g_ref[rs, :]).astype(o_ref.dtype)


def _xattn_main_side(cq_src, scg_src, kv):
    dh = MEM_HEAD_DIM
    (cq, cq_blk), (scg, scg_blk) = cq_src, scg_src
    qspec = lambda off: ((SEQ, dh), lambda i: (i // MEM_HEADS, off + i % MEM_HEADS))
    kvspec = lambda off: ((MEM_LEN, dh), lambda i: (i // MEM_HEADS, off + i % MEM_HEADS))
    return _side(
        lambda blk, *refs: _xattn_main_block(*refs), BATCH * MEM_HEADS,
        args=[cq, kv, kv, scg],
        in_specs=[qspec(cq_blk), kvspec(0), kvspec(MEM_HEADS), qspec(scg_blk)],
        out_specs=[qspec(0)],
        out_shape=[jax.ShapeDtypeStruct((M_MAIN, MEM_WIDTH), BF16)])


XATTN_TAIL_ROWS = 4


def _xattn_tail_block(q_ref, g_ref, k_ref, v_ref, o_ref):
    for r in range(q_ref.shape[0]):
        q = q_ref[r]
        sc = jnp.sum(k_ref[r] * q[None], axis=-1, keepdims=True) * (MEM_HEAD_DIM ** -0.5)
        e = jnp.exp(sc - jnp.max(sc, axis=0, keepdims=True))
        o_ref[r] = jnp.sum(e * v_ref[r], axis=0) / jnp.sum(e, axis=0) * g_ref[r]


def _xattn_tail_side(cq, scg, ck, cv):
    nb = XATTN_TAIL_ROWS
    shape3 = (M_TAIL, MEM_HEADS, MEM_HEAD_DIM)
    row = ((nb, MEM_HEADS, MEM_HEAD_DIM), lambda i: (i, 0, 0))
    kv = ((None, nb, MEM_LEN, MEM_HEADS, MEM_HEAD_DIM), lambda i: (0, i, 0, 0, 0))
    return _side(
        lambda blk, *refs: _xattn_tail_block(*refs), DEC_BATCH // nb,
        args=[cq.reshape(shape3), scg.reshape(shape3), ck, cv],
        in_specs=[row, row, kv, kv], out_specs=[row],
        out_shape=[jax.ShapeDtypeStruct(shape3, F32)])


def kernel(x_prompt, x_sample, state_ret, cache_mem_k, cache_mem_v, mem_prompt, g_pre, w_in, g_anorm,
           w_s, b_s, g_ret, g_mem, w_mem_kv, w_out_a, w_out_b, w_out_c, w_out, g_final):
    assert DEPTH == 1 and w_in.shape == (DEPTH, D_MODEL, IN_WIDTH)
    xp = x_prompt.reshape(M_MAIN, D_MODEL)
    xs = x_sample.reshape(M_TAIL, D_MODEL)
    win = w_in.reshape(D_MODEL, IN_WIDTH)

    h_m = _rmsnorm(xp, g_pre[0], 512, BF16)
    h_t = _rmsnorm(xs, g_pre[0], M_TAIL, BF16)
    cos_m, sin_m, cos_t, sin_t = _rope_tables()

    hm = _rmsnorm(mem_prompt.reshape(BATCH * MEM_LEN, D_MODEL), g_mem[0], 512, BF16)
    wkv = w_mem_kv.reshape(D_MODEL, 2 * MEM_WIDTH)
    (kv,) = _wres_matmul([(hm, None, wkv, 0)], [], [(F32, None)], n_cols=2 * MEM_WIDTH, tn=1024,
                         tm=512, sub=256, epilogue=_epi_identity, name="mem_kv")

    tn = tm = 1024

    def inproj(cols, n_cols, epilogue, out_dtypes, name, tn=tn, exts=(), **kw):
        return _wres_matmul([(h_m, h_t, win, c) for c in cols], list(exts), [out_dtypes],
                            n_cols=n_cols, tn=tn, tm=tm, sub=256, epilogue=epilogue, name=name, **kw)

    second = RET_WIDTH // tn
    id_m, id_t = inproj([((0, OFF_RV), (second, OFF_CQ))], RET_WIDTH + MEM_WIDTH, _epi_identity,
                        (BF16, F32), "inproj_v_cq")
    silu_m, silu_t = inproj([((0, OFF_RG), (second, OFF_CG))], RET_WIDTH + MEM_WIDTH, _epi_silu,
                            (BF16, F32), "inproj_rg_cg")
    gv_m, gv_t = inproj([OFF_AV], A_WIDTH, _epi_gelu, (BF16, F32), "inproj_gv")
    ug_m, ug_t = inproj([OFF_AU, OFF_AG], A_WIDTH, _epi_gelu_silu, (BF16, F32), "inproj_ug", tn=512)

    rope = [(cos_m, cos_t, "rope", 0), (sin_m, sin_t, "rope", 0)]
    q_m, q_t, a_m = inproj(
        [OFF_RQ], RET_WIDTH, functools.partial(_epi_rope, 1.0), (BF16, F32), "inproj_q", exts=rope,
        tn=512, side=_spatial_main_side(ug_m, (gv_m, 0), g_anorm[0], w_s[0], b_s[0]))
    a_t, vn_t = _spatial_tail(ug_t, gv_t, 0, g_anorm[0], w_s[0], b_s[0])
    mem_blk = RET_WIDTH // MEM_HEAD_DIM
    kt_m, k_t, c_m = inproj(
        [OFF_RK], RET_WIDTH, functools.partial(_epi_rope, RET_HEAD_DIM ** -0.5), (BF16, F32),
        "inproj_k", exts=rope, main_transposed=True,
        side=_xattn_main_side((id_m, mem_blk), (silu_m, mem_blk), kv),
        vmem_bytes=VMEM_HOST_BYTES)
    gs_m, gs_t, b_t, ret_s = inproj(
        [OFF_GATES], N_BRANCH * D_MODEL, _epi_sigmoid, (BF16, BF16), "inproj_gates",
        side=_ret_tail_side(q_t, k_t, (id_t, 0), (silu_t, 0), g_ret[0], state_ret),
        vmem_bytes=VMEM_HOST_BYTES)

    b_m, ret_p, c_t3 = _ret_main(
        q_m, kt_m, (id_m, 0), (silu_m, 0), g_ret[0],
        side=_xattn_tail_side(id_t[:, RET_WIDTH:], silu_t[:, RET_WIDTH:], cache_mem_k, cache_mem_v))
    c_t = c_t3.reshape(M_TAIL, MEM_WIDTH)

    merged_m, merged_t = _wres_matmul(
        [(a_m, a_t, w_out_a.reshape(A_WIDTH, D_MODEL), 0),
         (b_m, b_t, w_out_b.reshape(RET_WIDTH, D_MODEL), 0),
         (c_m, c_t, w_out_c.reshape(MEM_WIDTH, D_MODEL), 0)],
        [(gs_m, gs_t, "tile", b * D_MODEL) for b in range(N_BRANCH)],
        [(BF16, BF16)], n_cols=D_MODEL, tn=512, tm=1024, sub=256, epilogue=_epi_merge, name="merge")
    y_m, y_t = _wres_matmul(
        [(merged_m, merged_t, w_out.reshape(D_MODEL, D_MODEL), 0)],
        [(xp, xs, "tile", 0)], [(F32, F32)], n_cols=D_MODEL, tn=1024, tm=1024, sub=256,
        epilogue=_epi_residual, name="outproj", vmem_bytes=VMEM_HOST_BYTES)
    y_prompt = _rmsnorm(y_m, g_final, 512, F32).reshape(BATCH, SEQ, D_MODEL)
    y_sample = _rmsnorm(y_t, g_final, M_TAIL, F32).reshape(DEC_BATCH, DEC_SEQ, D_MODEL)

    mem_shape = (DEPTH, BATCH, MEM_LEN, MEM_HEADS, MEM_HEAD_DIM)
    return (y_prompt, y_sample, ret_p,
            kv[:, :MEM_WIDTH].reshape(mem_shape), kv[:, MEM_WIDTH:].reshape(mem_shape),
            ret_s,
            vn_t.reshape(DEPTH, DEC_BATCH, DEC_SEQ, A_WIDTH))
```

```python
import functools

import jax
import jax.numpy as jnp
from jax import lax
from jax.experimental import pallas as pl
from jax.experimental.pallas import tpu as pltpu

D_MODEL = 4096
BATCH = 4
SEQ = 2048
DEPTH = 1
DEC_BATCH = 128
DEC_SEQ = 1
PAST_LEN = 16384

CHUNK = 128
A_WIDTH = 2048
A_GROUPS = 4
RET_HEADS = 8
RET_HEAD_DIM = 256
RET_WIDTH = RET_HEADS * RET_HEAD_DIM
MEM_LEN = 256
MEM_HEADS = 4
MEM_HEAD_DIM = 256
MEM_WIDTH = MEM_HEADS * MEM_HEAD_DIM
N_BRANCH = 3
ROPE_BASE = 10000.0
EPS = 1e-6

OFF_AU = 0
OFF_AV = OFF_AU + A_WIDTH
OFF_AG = OFF_AV + A_WIDTH
OFF_RQ = OFF_AG + A_WIDTH
OFF_RK = OFF_RQ + RET_WIDTH
OFF_RV = OFF_RK + RET_WIDTH
OFF_RG = OFF_RV + RET_WIDTH
OFF_CQ = OFF_RG + RET_WIDTH
OFF_CG = OFF_CQ + MEM_WIDTH
OFF_GATES = OFF_CG + MEM_WIDTH
IN_WIDTH = OFF_GATES + N_BRANCH * D_MODEL

M_MAIN = BATCH * SEQ
M_TAIL = DEC_BATCH * DEC_SEQ
ROPE_HALF = RET_HEAD_DIM // 2

F32 = jnp.float32
BF16 = jnp.bfloat16

MIB = 1024 * 1024
VMEM_BUDGET_BYTES = 56 * MIB
VMEM_HOST_BYTES = 63 * MIB


def _params(semantics, vmem_bytes=VMEM_BUDGET_BYTES):
    return pltpu.CompilerParams(dimension_semantics=semantics, vmem_limit_bytes=vmem_bytes)


def _rmsnorm_kernel(x_ref, g_ref, o_ref):
    x = x_ref[...].astype(F32)
    y = x * lax.rsqrt(jnp.mean(x * x, axis=-1, keepdims=True) + EPS)
    o_ref[...] = (y * g_ref[...].astype(F32)).astype(o_ref.dtype)


def _rmsnorm(x, g, rows, out_dtype):
    m, d = x.shape
    return pl.pallas_call(
        _rmsnorm_kernel,
        grid=(m // rows,),
        in_specs=[pl.BlockSpec((rows, d), lambda i: (i, 0)),
                  pl.BlockSpec((1, d), lambda i: (0, 0))],
        out_specs=pl.BlockSpec((rows, d), lambda i: (i, 0)),
        out_shape=jax.ShapeDtypeStruct((m, d), out_dtype),
        compiler_params=_params(("arbitrary",)),
        name="rmsnorm",
    )(x, g.reshape(1, d))


def _rope_kernel(cm_ref, sm_ref, ct_ref, st_ref):
    def table(rows, pos):
        j = lax.broadcasted_iota(jnp.int32, (rows, ROPE_HALF), 1).astype(F32)
        inv = ROPE_BASE ** (-j / ROPE_HALF)
        return pos.astype(F32) * inv

    ang = table(SEQ, lax.broadcasted_iota(jnp.int32, (SEQ, ROPE_HALF), 0))
    cm_ref[...] = jnp.cos(ang)
    sm_ref[...] = jnp.sin(ang)
    r = lax.broadcasted_iota(jnp.int32, (M_TAIL, ROPE_HALF), 0)
    t = jnp.zeros_like(r) if DEC_SEQ == 1 else lax.rem(r, DEC_SEQ)
    ang_t = table(M_TAIL, PAST_LEN + t)
    ct_ref[...] = jnp.cos(ang_t)
    st_ref[...] = jnp.sin(ang_t)


def _rope_tables():
    return pl.pallas_call(
        _rope_kernel,
        out_shape=(jax.ShapeDtypeStruct((SEQ, ROPE_HALF), F32),
                   jax.ShapeDtypeStruct((SEQ, ROPE_HALF), F32),
                   jax.ShapeDtypeStruct((M_TAIL, ROPE_HALF), F32),
                   jax.ShapeDtypeStruct((M_TAIL, ROPE_HALF), F32)),
        name="rope_tables",
    )()


CAST_ROWS = 256


def _wres_kernel(*refs, lhs_of, w_cols, n_tiles, n_ext, n_out, mt, has_tail, sub, epilogue,
                 side_fn, n_side_in, n_side_out, main_transposed):
    refs = list(refs)
    n_pairs, n_lhs = len(lhs_of), max(lhs_of) + 1

    def take(k):
        out = refs[:k]
        del refs[:k]
        return out

    lhs_main = take(n_lhs)
    lhs_tail = take(n_lhs) if has_tail else []
    w_hbm = take(n_pairs)
    ext_main = take(n_ext)
    ext_tail = take(n_ext) if has_tail else []
    side_in = take(n_side_in)
    out_main = take(n_out)
    out_tail = take(n_out) if has_tail else []
    side_out = take(n_side_out)
    wf = take(n_pairs)
    wb = take(n_pairs)
    (sem,) = take(1)

    n = pl.program_id(0)
    m = pl.program_id(1)
    tn = wb[0].shape[1]

    def w_copy(p, tile):
        col = w_cols[p][0][1] + tile * tn
        for first_tile, first_col in w_cols[p][1:]:
            col = jnp.where(tile >= first_tile, first_col + (tile - first_tile) * tn, col)
        col = pl.multiple_of(col, tn)
        return pltpu.make_async_copy(w_hbm[p].at[:, pl.ds(col, tn)], wf[p], sem.at[p])

    @pl.when(m == 0)
    def _weights():
        @pl.when(n == 0)
        def _first():
            for p in range(n_pairs):
                w_copy(p, 0).start()

        for p in range(n_pairs):
            w_copy(p, n).wait()

            def body(i, carry, wf_ref=wf[p], wb_ref=wb[p]):
                r = pl.multiple_of(i * CAST_ROWS, CAST_ROWS)
                wb_ref[pl.ds(r, CAST_ROWS), :] = wf_ref[pl.ds(r, CAST_ROWS), :].astype(BF16)
                return carry
            lax.fori_loop(0, wf[p].shape[0] // CAST_ROWS, body, 0)

        @pl.when(n + 1 < n_tiles)
        def _next():
            for p in range(n_pairs):
                w_copy(p, n + 1).start()

    def body(lhs, ext, outs, transposed=False):
        for c in range(0, tn, sub):
            accs = [jnp.dot(lhs[i][...].astype(BF16), b[:, c:c + sub], preferred_element_type=F32)
                    for i, b in zip(lhs_of, wb)]
            res = epilogue(accs, ext, c, sub, n)
            for o, r in zip(outs, res):
                if transposed:
                    o[c:c + sub, :] = r.T.astype(o.dtype)
                else:
                    o[:, c:c + sub] = r.astype(o.dtype)

    body(lhs_main, ext_main, out_main, main_transposed)
    if side_fn is not None:
        side_fn(n, m, *side_in, *side_out)
    if has_tail:
        @pl.when(m == mt - 1)
        def _tail():
            body(lhs_tail, ext_tail, out_tail)


def _wres_matmul(pairs, exts, outs, *, n_cols, tn, tm, sub, epilogue, name, side=None,
                 vmem_bytes=VMEM_BUDGET_BYTES, main_transposed=False):
    m_main = pairs[0][0].shape[0]
    has_tail = pairs[0][1] is not None
    mt = m_main // tm
    n_tiles = n_cols // tn
    grid = (n_tiles, mt)

    lhs, lhs_of = [], []
    for lm, lt, _, _ in pairs:
        ids = [i for i, (a, _) in enumerate(lhs) if a is lm]
        if not ids:
            lhs.append((lm, lt))
        lhs_of.append(ids[0] if ids else len(lhs) - 1)

    in_specs, args = [], []
    for lm, _ in lhs:
        in_specs.append(pl.BlockSpec((tm, lm.shape[1]), lambda n, m: (m, 0)))
        args.append(lm)
    if has_tail:
        for _, lt in lhs:
            in_specs.append(pl.BlockSpec(lt.shape, lambda n, m: (0, 0)))
            args.append(lt)
    w_cols = []
    for _, _, w, cols in pairs:
        segs = ((0, cols),) if isinstance(cols, int) else tuple(cols)
        for i, (first_tile, first_col) in enumerate(segs):
            last_tile = segs[i + 1][0] if i + 1 < len(segs) else n_tiles
            assert first_col % tn == 0 and first_col + (last_tile - first_tile) * tn <= w.shape[1]
        w_cols.append(segs)
        in_specs.append(pl.BlockSpec(memory_space=pl.ANY))
        args.append(w)

    def ext_spec(arr, kind, off, tail):
        rows = arr.shape[0] if tail else tm
        if kind == "tile":
            assert off % tn == 0
            if tail:
                return pl.BlockSpec((rows, tn), lambda n, m, o=off // tn: (0, o + n))
            return pl.BlockSpec((rows, tn), lambda n, m, o=off // tn: (m, o + n))
        assert kind == "rope"
        if tail:
            return pl.BlockSpec(arr.shape, lambda n, m: (0, 0))
        per = arr.shape[0] // tm
        return pl.BlockSpec((tm, arr.shape[1]), lambda n, m: (lax.rem(m, per), 0))

    for em, _, kind, off in exts:
        in_specs.append(ext_spec(em, kind, off, False))
        args.append(em)
    if has_tail:
        for _, et, kind, off in exts:
            in_specs.append(ext_spec(et, kind, off, True))
            args.append(et)

    out_specs, out_shape = [], []
    for dm, _ in outs:
        if main_transposed:
            out_specs.append(pl.BlockSpec((tn, tm), lambda n, m: (n, m)))
            out_shape.append(jax.ShapeDtypeStruct((n_cols, m_main), dm))
        else:
            out_specs.append(pl.BlockSpec((tm, tn), lambda n, m: (m, n)))
            out_shape.append(jax.ShapeDtypeStruct((m_main, n_cols), dm))
    if has_tail:
        m_tail = pairs[0][1].shape[0]
        for _, dt in outs:
            out_specs.append(pl.BlockSpec((m_tail, tn), lambda n, m: (0, n)))
            out_shape.append(jax.ShapeDtypeStruct((m_tail, n_cols), dt))

    if side is not None:
        assert n_tiles * mt >= side["n_blocks"]
        step = lambda n, m: jnp.minimum(n * mt + m, side["n_blocks"] - 1)
        in_specs += [pl.BlockSpec(shape, lambda n, m, f=f: f(step(n, m)))
                     for shape, f in side["in_specs"]]
        args += side["args"]
        out_specs += [pl.BlockSpec(shape, lambda n, m, f=f: f(step(n, m)))
                      for shape, f in side["out_specs"]]
        out_shape += side["out_shape"]
        side_fn = lambda n, m, *refs: side["fn"](step(n, m), *refs)
    else:
        side_fn = None

    scratch = ([pltpu.VMEM((w.shape[0], tn), F32) for _, _, w, _ in pairs]
               + [pltpu.VMEM((w.shape[0], tn), BF16) for _, _, w, _ in pairs]
               + [pltpu.SemaphoreType.DMA((len(pairs),))])
    kern = functools.partial(_wres_kernel, lhs_of=tuple(lhs_of), w_cols=tuple(w_cols),
                             n_tiles=n_tiles, n_ext=len(exts), n_out=len(outs),
                             mt=mt, has_tail=has_tail, sub=sub, epilogue=epilogue,
                             side_fn=side_fn,
                             n_side_in=len(side["args"]) if side else 0,
                             n_side_out=len(side["out_shape"]) if side else 0,
                             main_transposed=main_transposed)
    return pl.pallas_call(
        kern, grid=grid, in_specs=in_specs, out_specs=out_specs, out_shape=out_shape,
        scratch_shapes=scratch, compiler_params=_params(("arbitrary", "arbitrary"), vmem_bytes),
        name=name,
    )(*args)


def _side(fn, n_blocks, args, in_specs, out_specs, out_shape):
    return dict(fn=fn, n_blocks=n_blocks, args=args, in_specs=in_specs, out_specs=out_specs,
                out_shape=out_shape)


def _sigmoid(x):
    return 0.5 * jnp.tanh(0.5 * x) + 0.5


def _epi_identity(accs, ext, c, sub, n):
    return [accs[0]]


def _epi_sigmoid(accs, ext, c, sub, n):
    return [_sigmoid(accs[0])]


GELU_C1 = (2.0 / 3.141592653589793) ** 0.5
GELU_C2 = GELU_C1 * 0.044715


def _gelu_tanh_arg(x):
    return x * (GELU_C1 + GELU_C2 * (x * x))


def _epi_gelu_silu(accs, ext, c, sub, n):
    u, g = accs
    return [(0.25 * (u * g)) * ((1.0 + jnp.tanh(_gelu_tanh_arg(u))) * (1.0 + jnp.tanh(0.5 * g)))]


def _epi_gelu(accs, ext, c, sub, n):
    x = accs[0]
    return [(0.5 * x) * (1.0 + jnp.tanh(_gelu_tanh_arg(x)))]


def _epi_silu(accs, ext, c, sub, n):
    return [accs[0] * _sigmoid(accs[0])]


def _epi_rope(scale, accs, ext, c, sub, n):
    assert sub == RET_HEAD_DIM
    x = accs[0]
    x1, x2 = x[:, :ROPE_HALF], x[:, ROPE_HALF:]
    cos, sin = ext[0][...], ext[1][...]
    out = jnp.concatenate([x1 * cos - x2 * sin, x1 * sin + x2 * cos], axis=-1)
    return [out if scale == 1.0 else out * scale]


def _epi_merge(accs, ext, c, sub, n):
    g0, g1, g2 = (e[:, c:c + sub] for e in ext)
    return [g0 * accs[0] + g1 * accs[1] + g2 * accs[2]]


def _epi_residual(accs, ext, c, sub, n):
    return [ext[0][:, c:c + sub] + accs[0]]


SPATIAL_ROWS = 256


def _layer_norm(x, g):
    mu = jnp.mean(x, axis=-1, keepdims=True)
    var = jnp.mean(jnp.square(x - mu), axis=-1, keepdims=True)
    return (x - mu) * lax.rsqrt(var + EPS) * g


def _spatial_main_block(ug_ref, gv_ref, ga_ref, ws_ref, bst_ref, o_ref):
    rows = gv_ref.shape[0]
    gw = A_WIDTH // A_GROUPS
    vn = _layer_norm(gv_ref[...].astype(F32), ga_ref[...]).astype(BF16)
    ri = lax.broadcasted_iota(jnp.int32, (CHUNK, CHUNK), 0)
    ci = lax.broadcasted_iota(jnp.int32, (CHUNK, CHUNK), 1)
    for g in range(A_GROUPS):
        w = jnp.where(ri >= ci, ws_ref[g], 0.0).astype(BF16)
        bias = bst_ref[:, g:g + 1]
        for c in range(rows // CHUNK):
            rs = slice(c * CHUNK, (c + 1) * CHUNK)
            cs = slice(g * gw, (g + 1) * gw)
            sp = jnp.dot(w, vn[rs, cs], preferred_element_type=F32) + bias
            o_ref[rs, cs] = (ug_ref[rs, cs] * sp).astype(o_ref.dtype)


def _spatial_main_side(ug, gv_src, g_anorm, w_s, b_s):
    gv, gv_blk = gv_src
    rows = SPATIAL_ROWS
    return _side(
        lambda blk, *refs: _spatial_main_block(*refs), M_MAIN // rows,
        args=[ug, gv, g_anorm.reshape(1, A_WIDTH), w_s, b_s.T],
        in_specs=[((rows, A_WIDTH), lambda i: (i, 0)),
                  ((rows, A_WIDTH), lambda i: (i, gv_blk)),
                  ((1, A_WIDTH), lambda i: (0, 0)),
                  ((A_GROUPS, CHUNK, CHUNK), lambda i: (0, 0, 0)),
                  ((CHUNK, A_GROUPS), lambda i: (0, 0))],
        out_specs=[((rows, A_WIDTH), lambda i: (i, 0))],
        out_shape=[jax.ShapeDtypeStruct((M_MAIN, A_WIDTH), BF16)])


def _spatial_tail_kernel(ug_ref, gv_ref, ga_ref, ws_ref, bs_ref, o_ref, vn_ref, *, gv_col):
    gw = A_WIDTH // A_GROUPS
    vn = _layer_norm(gv_ref[:, gv_col:gv_col + A_WIDTH], ga_ref[...])
    vn_ref[...] = vn
    for g in range(A_GROUPS):
        cs = slice(g * gw, (g + 1) * gw)
        sp = vn[:, cs] * ws_ref[g, 0:1, 0:1] + bs_ref[g:g + 1, 0:1]
        o_ref[:, cs] = (ug_ref[:, cs] * sp).astype(o_ref.dtype)


def _spatial_tail(ug, gv, gv_col, g_anorm, w_s, b_s):
    m = ug.shape[0]
    return pl.pallas_call(
        functools.partial(_spatial_tail_kernel, gv_col=gv_col),
        out_shape=(jax.ShapeDtypeStruct((m, A_WIDTH), BF16),
                   jax.ShapeDtypeStruct((m, A_WIDTH), F32)),
        compiler_params=_params(None),
        name="spatial_tail",
    )(ug, gv, g_anorm.reshape(1, A_WIDTH), w_s, b_s)


def _log_gamma(shape, head):
    return jnp.log1p(-jnp.exp2(jnp.full(shape, -5.0, F32) - head.astype(F32)))


def _group_norm(o, g):
    mu = jnp.mean(o, axis=-1, keepdims=True)
    var = jnp.mean(jnp.square(o - mu), axis=-1, keepdims=True)
    return (o - mu) * lax.rsqrt(var + EPS) * g


def _ret_main_kernel(*refs, side_fn, n_side_in):
    q_ref, kt_ref, v_ref, sg_ref, gr_ref = refs[:5]
    side_in = refs[5:5 + n_side_in]
    o_ref, r_ref = refs[5 + n_side_in:7 + n_side_in]
    side_out = refs[7 + n_side_in:]
    head = pl.program_id(1)
    L, dk = CHUNK, RET_HEAD_DIM
    ri = lax.broadcasted_iota(jnp.int32, (L, L), 0).astype(F32)
    ci = lax.broadcasted_iota(jnp.int32, (L, L), 1).astype(F32)
    diff = ri - ci
    decay_in = jnp.where(diff >= 0, jnp.exp(_log_gamma((L, L), head) * jnp.maximum(diff, 0.0)), 0.0)
    rw = lax.broadcasted_iota(jnp.int32, (L, dk), 0).astype(F32)
    decay_q = jnp.exp(_log_gamma((L, dk), head) * (rw + 1.0))
    cw = lax.broadcasted_iota(jnp.int32, (dk, L), 1).astype(F32)
    decay_kt = jnp.exp(_log_gamma((dk, L), head) * (L - 1.0 - cw))
    decay_blk = jnp.exp(_log_gamma((dk, dk), head) * L)
    gr = gr_ref[...]

    R = jnp.zeros((dk, dk), F32)
    for i in range(SEQ // L):
        rs = slice(i * L, (i + 1) * L)
        qi, kti, vi = q_ref[rs, :], kt_ref[:, rs], v_ref[rs, :]
        s = jnp.dot(qi, kti, preferred_element_type=F32) * decay_in
        o = (jnp.dot(s.astype(BF16), vi, preferred_element_type=F32)
             + jnp.dot(qi, R.astype(BF16), preferred_element_type=F32) * decay_q)
        kdt = (kti.astype(F32) * decay_kt).astype(BF16)
        R = R * decay_blk + jnp.dot(kdt, vi, preferred_element_type=F32)
        o_ref[rs, :] = (_group_norm(o, gr) * sg_ref[rs, :]).astype(o_ref.dtype)
    r_ref[...] = R
    side_fn(pl.program_id(0) * RET_HEADS + head, *side_in, *side_out)


def _ret_main(q, kt, v_src, sg_src, g_ret, side):
    dk = RET_HEAD_DIM
    (v, v_blk), (sg, sg_blk) = v_src, sg_src
    blk = lambda off: pl.BlockSpec((SEQ, dk), lambda b, h: (b, off + h))
    step = lambda b, h: b * RET_HEADS + h
    assert BATCH * RET_HEADS == side["n_blocks"]
    in_specs = [blk(0), pl.BlockSpec((dk, SEQ), lambda b, h: (h, b)), blk(v_blk), blk(sg_blk),
                pl.BlockSpec((1, dk), lambda b, h: (0, h))]
    in_specs += [pl.BlockSpec(shape, lambda b, h, f=f: f(step(b, h))) for shape, f in side["in_specs"]]
    out_specs = [blk(0), pl.BlockSpec((None, None, None, dk, dk), lambda b, h: (0, b, h, 0, 0))]
    out_specs += [pl.BlockSpec(shape, lambda b, h, f=f: f(step(b, h))) for shape, f in side["out_specs"]]
    out_shape = [jax.ShapeDtypeStruct((M_MAIN, RET_WIDTH), BF16),
                 jax.ShapeDtypeStruct((DEPTH, BATCH, RET_HEADS, dk, dk), F32)] + side["out_shape"]
    kern = functools.partial(_ret_main_kernel, side_fn=side["fn"], n_side_in=len(side["args"]))
    return pl.pallas_call(
        kern, grid=(BATCH, RET_HEADS), in_specs=in_specs, out_specs=out_specs, out_shape=out_shape,
        compiler_params=_params(("arbitrary", "arbitrary")), name="ret_main",
    )(q, kt, v, sg, g_ret.reshape(1, RET_WIDTH), *side["args"])


RET_TAIL_ROWS = 16


def _ret_tail_block(blk, q_ref, k_ref, v_ref, sg_ref, gr_ref, s_ref, o_ref, so_ref):
    head = blk % RET_HEADS
    nb, dk = q_ref.shape
    gam_row = jnp.exp(_log_gamma((nb, dk), head))
    gam_st = jnp.exp(_log_gamma((dk, dk), head))
    q, k, v = q_ref[...], k_ref[...], v_ref[...]
    qt, kt = q.T, k.T
    rows = []
    for r in range(nb):
        R = s_ref[r]
        so_ref[r] = R * gam_st + kt[:, r:r + 1] * v[r:r + 1, :]
        rows.append(jnp.sum(qt[:, r:r + 1] * R, axis=0, keepdims=True))
    qr = jnp.concatenate(rows, axis=0)
    qk = jnp.sum(q * k, axis=-1, keepdims=True)
    o = qk * v + qr * gam_row
    o_ref[...] = _group_norm(o, gr_ref[...]) * sg_ref[...]


def _ret_tail_side(q, k, v_src, sg_src, g_ret, state):
    dk, nb = RET_HEAD_DIM, RET_TAIL_ROWS
    (v, v_blk), (sg, sg_blk) = v_src, sg_src
    row = lambda off: ((nb, dk), lambda i: (i // RET_HEADS, off + i % RET_HEADS))
    st = ((None, nb, None, dk, dk), lambda i: (0, i // RET_HEADS, i % RET_HEADS, 0, 0))
    return _side(
        _ret_tail_block, (DEC_BATCH // nb) * RET_HEADS,
        args=[q, k, v, sg, g_ret.reshape(1, RET_WIDTH), state],
        in_specs=[row(0), row(0), row(v_blk), row(sg_blk),
                  ((1, dk), lambda i: (0, i % RET_HEADS)), st],
        out_specs=[row(0), st],
        out_shape=[jax.ShapeDtypeStruct((M_TAIL, RET_WIDTH), F32),
                   jax.ShapeDtypeStruct(state.shape, F32)])


XATTN_ROWS = 512


def _xattn_main_block(q_ref, k_ref, v_ref, g_ref, o_ref):
    k = k_ref[...].astype(BF16)
    v = v_ref[...].astype(BF16)
    for t in range(q_ref.shape[0] // XATTN_ROWS):
        rs = slice(t * XATTN_ROWS, (t + 1) * XATTN_ROWS)
        sc = lax.dot_general(q_ref[rs, :], k, (((1,), (1,)), ((), ())), preferred_element_type=F32)
        sc = sc * (MEM_HEAD_DIM ** -0.5)
        e = jnp.exp(sc - jnp.max(sc, axis=-1, keepdims=True))
        p = e * (1.0 / jnp.sum(e, axis=-1, keepdims=True))
        om = jnp.dot(p.astype(BF16), v, preferred_element_type=F32)
        o_ref[rs, :] = (om * g_ref[rs, :]).astype(o_ref.dtype)


def _xattn_main_side(cq_src, scg_src, kv):
    dh = MEM_HEAD_DIM
    (cq, cq_blk), (scg, scg_blk) = cq_src, scg_src
    qspec = lambda off: ((SEQ, dh), lambda i: (i // MEM_HEADS, off + i % MEM_HEADS))
    kvspec = lambda off: ((MEM_LEN, dh), lambda i: (i // MEM_HEADS, off + i % MEM_HEADS))
    return _side(
        lambda blk, *refs: _xattn_main_block(*refs), BATCH * MEM_HEADS,
        args=[cq, kv, kv, scg],
        in_specs=[qspec(cq_blk), kvspec(0), kvspec(MEM_HEADS), qspec(scg_blk)],
        out_specs=[qspec(0)],
        out_shape=[jax.ShapeDtypeStruct((M_MAIN, MEM_WIDTH), BF16)])


XATTN_TAIL_ROWS = 4


def _xattn_tail_block(q_ref, g_ref, k_ref, v_ref, o_ref):
    for r in range(q_ref.shape[0]):
        q = q_ref[r] * (MEM_HEAD_DIM ** -0.5)
        sc = jnp.sum(k_ref[r] * q[None], axis=-1, keepdims=True)
        e = jnp.exp(sc - jnp.max(sc, axis=0, keepdims=True))
        o_ref[r] = jnp.sum(e * v_ref[r], axis=0) / jnp.sum(e, axis=0) * g_ref[r]


def _xattn_tail_side(cq, scg, ck, cv):
    nb = XATTN_TAIL_ROWS
    shape3 = (M_TAIL, MEM_HEADS, MEM_HEAD_DIM)
    row = ((nb, MEM_HEADS, MEM_HEAD_DIM), lambda i: (i, 0, 0))
    kv = ((None, nb, MEM_LEN, MEM_HEADS, MEM_HEAD_DIM), lambda i: (0, i, 0, 0, 0))
    return _side(
        lambda blk, *refs: _xattn_tail_block(*refs), DEC_BATCH // nb,
        args=[cq.reshape(shape3), scg.reshape(shape3), ck, cv],
        in_specs=[row, row, kv, kv], out_specs=[row],
        out_shape=[jax.ShapeDtypeStruct(shape3, F32)])


def kernel(x_prompt, x_sample, state_ret, cache_mem_k, cache_mem_v, mem_prompt, g_pre, w_in, g_anorm,
           w_s, b_s, g_ret, g_mem, w_mem_kv, w_out_a, w_out_b, w_out_c, w_out, g_final):
    assert DEPTH == 1 and w_in.shape == (DEPTH, D_MODEL, IN_WIDTH)
    xp = x_prompt.reshape(M_MAIN, D_MODEL)
    xs = x_sample.reshape(M_TAIL, D_MODEL)
    win = w_in.reshape(D_MODEL, IN_WIDTH)

    h_m = _rmsnorm(xp, g_pre[0], 512, BF16)
    h_t = _rmsnorm(xs, g_pre[0], M_TAIL, BF16)
    cos_m, sin_m, cos_t, sin_t = _rope_tables()

    hm = _rmsnorm(mem_prompt.reshape(BATCH * MEM_LEN, D_MODEL), g_mem[0], 512, BF16)
    wkv = w_mem_kv.reshape(D_MODEL, 2 * MEM_WIDTH)
    (kv,) = _wres_matmul([(hm, None, wkv, 0)], [], [(F32, None)], n_cols=2 * MEM_WIDTH, tn=1024,
                         tm=512, sub=256, epilogue=_epi_identity, name="mem_kv")

    tn = tm = 1024

    def inproj(cols, n_cols, epilogue, out_dtypes, name, tn=tn, exts=(), **kw):
        return _wres_matmul([(h_m, h_t, win, c) for c in cols], list(exts), [out_dtypes],
                            n_cols=n_cols, tn=tn, tm=tm, sub=256, epilogue=epilogue, name=name, **kw)

    second = RET_WIDTH // tn
    id_m, id_t = inproj([((0, OFF_RV), (second, OFF_CQ))], RET_WIDTH + MEM_WIDTH, _epi_identity,
                        (BF16, F32), "inproj_v_cq")
    silu_m, silu_t = inproj([((0, OFF_RG), (second, OFF_CG))], RET_WIDTH + MEM_WIDTH, _epi_silu,
                            (BF16, F32), "inproj_rg_cg")
    gv_m, gv_t = inproj([OFF_AV], A_WIDTH, _epi_gelu, (BF16, F32), "inproj_gv")
    ug_m, ug_t = inproj([OFF_AU, OFF_AG], A_WIDTH, _epi_gelu_silu, (BF16, F32), "inproj_ug", tn=512)

    rope = [(cos_m, cos_t, "rope", 0), (sin_m, sin_t, "rope", 0)]
    q_m, q_t, a_m = inproj(
        [OFF_RQ], RET_WIDTH, functools.partial(_epi_rope, 1.0), (BF16, F32), "inproj_q", exts=rope,
        tn=512, side=_spatial_main_side(ug_m, (gv_m, 0), g_anorm[0], w_s[0], b_s[0]))
    a_t, vn_t = _spatial_tail(ug_t, gv_t, 0, g_anorm[0], w_s[0], b_s[0])
    mem_blk = RET_WIDTH // MEM_HEAD_DIM
    kt_m, k_t, c_m = inproj(
        [OFF_RK], RET_WIDTH, functools.partial(_epi_rope, RET_HEAD_DIM ** -0.5), (BF16, F32),
        "inproj_k", exts=rope, main_transposed=True,
        side=_xattn_main_side((id_m, mem_blk), (silu_m, mem_blk), kv),
        vmem_bytes=VMEM_HOST_BYTES)
    gs_m, gs_t, b_t, ret_s = inproj(
        [OFF_GATES], N_BRANCH * D_MODEL, _epi_sigmoid, (BF16, BF16), "inproj_gates",
        side=_ret_tail_side(q_t, k_t, (id_t, 0), (silu_t, 0), g_ret[0], state_ret),
        vmem_bytes=VMEM_HOST_BYTES)

    b_m, ret_p, c_t3 = _ret_main(
        q_m, kt_m, (id_m, 0), (silu_m, 0), g_ret[0],
        side=_xattn_tail_side(id_t[:, RET_WIDTH:], silu_t[:, RET_WIDTH:], cache_mem_k, cache_mem_v))
    c_t = c_t3.reshape(M_TAIL, MEM_WIDTH)

    merged_m, merged_t = _wres_matmul(
        [(a_m, a_t, w_out_a.reshape(A_WIDTH, D_MODEL), 0),
         (b_m, b_t, w_out_b.reshape(RET_WIDTH, D_MODEL), 0),
         (c_m, c_t, w_out_c.reshape(MEM_WIDTH, D_MODEL), 0)],
        [(gs_m, gs_t, "tile", b * D_MODEL) for b in range(N_BRANCH)],
        [(BF16, BF16)], n_cols=D_MODEL, tn=512, tm=1024, sub=256, epilogue=_epi_merge, name="merge")
    y_m, y_t = _wres_matmul(
        [(merged_m, merged_t, w_out.reshape(D_MODEL, D_MODEL), 0)],
        [(xp, xs, "tile", 0)], [(F32, F32)], n_cols=D_MODEL, tn=1024, tm=1024, sub=256,
        epilogue=_epi_residual, name="outproj", vmem_bytes=VMEM_HOST_BYTES)
    y_prompt = _rmsnorm(y_m, g_final, 512, F32).reshape(BATCH, SEQ, D_MODEL)
    y_sample = _rmsnorm(y_t, g_final, M_TAIL, F32).reshape(DEC_BATCH, DEC_SEQ, D_MODEL)

    mem_shape = (DEPTH, BATCH, MEM_LEN, MEM_HEADS, MEM_HEAD_DIM)
    return (y_prompt, y_sample, ret_p,
            kv[:, :MEM_WIDTH].reshape(mem_shape), kv[:, MEM_WIDTH:].reshape(mem_shape),
            ret_s,
            vn_t.reshape(DEPTH, DEC_BATCH, DEC_SEQ, A_WIDTH))
```

```python
import functools

import jax
import jax.numpy as jnp
from jax import lax
from jax.experimental import pallas as pl
from jax.experimental.pallas import tpu as pltpu

D_MODEL = 4096
BATCH = 4
SEQ = 2048
DEPTH = 1
DEC_BATCH = 128
DEC_SEQ = 1
PAST_LEN = 16384

CHUNK = 128
A_WIDTH = 2048
A_GROUPS = 4
RET_HEADS = 8
RET_HEAD_DIM = 256
RET_WIDTH = RET_HEADS * RET_HEAD_DIM
MEM_LEN = 256
MEM_HEADS = 4
MEM_HEAD_DIM = 256
MEM_WIDTH = MEM_HEADS * MEM_HEAD_DIM
N_BRANCH = 3
ROPE_BASE = 10000.0
EPS = 1e-6

OFF_AU = 0
OFF_AV = OFF_AU + A_WIDTH
OFF_AG = OFF_AV + A_WIDTH
OFF_RQ = OFF_AG + A_WIDTH
OFF_RK = OFF_RQ + RET_WIDTH
OFF_RV = OFF_RK + RET_WIDTH
OFF_RG = OFF_RV + RET_WIDTH
OFF_CQ = OFF_RG + RET_WIDTH
OFF_CG = OFF_CQ + MEM_WIDTH
OFF_GATES = OFF_CG + MEM_WIDTH
IN_WIDTH = OFF_GATES + N_BRANCH * D_MODEL

M_MAIN = BATCH * SEQ
M_TAIL = DEC_BATCH * DEC_SEQ
ROPE_HALF = RET_HEAD_DIM // 2

F32 = jnp.float32
BF16 = jnp.bfloat16

MIB = 1024 * 1024
VMEM_BUDGET_BYTES = 56 * MIB
VMEM_HOST_BYTES = 63 * MIB


def _params(semantics, vmem_bytes=VMEM_BUDGET_BYTES):
    return pltpu.CompilerParams(dimension_semantics=semantics, vmem_limit_bytes=vmem_bytes)


def _rmsnorm_kernel(x_ref, g_ref, o_ref):
    x = x_ref[...].astype(F32)
    y = x * lax.rsqrt(jnp.mean(x * x, axis=-1, keepdims=True) + EPS)
    o_ref[...] = (y * g_ref[...].astype(F32)).astype(o_ref.dtype)


def _rmsnorm(x, g, rows, out_dtype, out_3d=False):
    m, d = x.shape[0], x.shape[-1]
    spec3 = pl.BlockSpec((rows, None, d), lambda i: (i, 0, 0))
    spec2 = pl.BlockSpec((rows, d), lambda i: (i, 0))
    return pl.pallas_call(
        _rmsnorm_kernel,
        grid=(m // rows,),
        in_specs=[spec3 if x.ndim == 3 else spec2, pl.BlockSpec((1, d), lambda i: (0, 0))],
        out_specs=spec3 if out_3d else spec2,
        out_shape=jax.ShapeDtypeStruct((m, 1, d) if out_3d else (m, d), out_dtype),
        compiler_params=_params(("arbitrary",)),
        name="rmsnorm",
    )(x, g.reshape(1, d))


def _rope_kernel(cm_ref, sm_ref, ct_ref, st_ref):
    def table(rows, pos):
        j = lax.broadcasted_iota(jnp.int32, (rows, ROPE_HALF), 1).astype(F32)
        inv = ROPE_BASE ** (-j / ROPE_HALF)
        return pos.astype(F32) * inv

    ang = table(SEQ, lax.broadcasted_iota(jnp.int32, (SEQ, ROPE_HALF), 0))
    cm_ref[...] = jnp.cos(ang)
    sm_ref[...] = jnp.sin(ang)
    r = lax.broadcasted_iota(jnp.int32, (M_TAIL, ROPE_HALF), 0)
    t = jnp.zeros_like(r) if DEC_SEQ == 1 else lax.rem(r, DEC_SEQ)
    ang_t = table(M_TAIL, PAST_LEN + t)
    ct_ref[...] = jnp.cos(ang_t)
    st_ref[...] = jnp.sin(ang_t)


def _rope_tables():
    return pl.pallas_call(
        _rope_kernel,
        out_shape=(jax.ShapeDtypeStruct((SEQ, ROPE_HALF), F32),
                   jax.ShapeDtypeStruct((SEQ, ROPE_HALF), F32),
                   jax.ShapeDtypeStruct((M_TAIL, ROPE_HALF), F32),
                   jax.ShapeDtypeStruct((M_TAIL, ROPE_HALF), F32)),
        name="rope_tables",
    )()


CAST_ROWS = 256


def _wres_kernel(*refs, lhs_of, w_cols, n_tiles, n_ext, n_out, mt, has_tail, sub, epilogue,
                 side_fn, n_side_in, n_side_out, main_transposed):
    refs = list(refs)
    n_pairs, n_lhs = len(lhs_of), max(lhs_of) + 1

    def take(k):
        out = refs[:k]
        del refs[:k]
        return out

    lhs_main = take(n_lhs)
    lhs_tail = take(n_lhs) if has_tail else []
    w_hbm = take(n_pairs)
    ext_main = take(n_ext)
    ext_tail = take(n_ext) if has_tail else []
    side_in = take(n_side_in)
    out_main = take(n_out)
    out_tail = take(n_out) if has_tail else []
    side_out = take(n_side_out)
    wf = take(n_pairs)
    wb = take(n_pairs)
    (sem,) = take(1)

    n = pl.program_id(0)
    m = pl.program_id(1)
    tn = wb[0].shape[1]

    def w_copy(p, tile):
        col = w_cols[p][0][1] + tile * tn
        for first_tile, first_col in w_cols[p][1:]:
            col = jnp.where(tile >= first_tile, first_col + (tile - first_tile) * tn, col)
        col = pl.multiple_of(col, tn)
        return pltpu.make_async_copy(w_hbm[p].at[:, pl.ds(col, tn)], wf[p], sem.at[p])

    @pl.when(m == 0)
    def _weights():
        @pl.when(n == 0)
        def _first():
            for p in range(n_pairs):
                w_copy(p, 0).start()

        for p in range(n_pairs):
            w_copy(p, n).wait()

            def body(i, carry, wf_ref=wf[p], wb_ref=wb[p]):
                r = pl.multiple_of(i * CAST_ROWS, CAST_ROWS)
                wb_ref[pl.ds(r, CAST_ROWS), :] = wf_ref[pl.ds(r, CAST_ROWS), :].astype(BF16)
                return carry
            lax.fori_loop(0, wf[p].shape[0] // CAST_ROWS, body, 0)

        @pl.when(n + 1 < n_tiles)
        def _next():
            for p in range(n_pairs):
                w_copy(p, n + 1).start()

    def body(lhs, ext, outs, transposed=False):
        for c in range(0, tn, sub):
            accs = [jnp.dot(lhs[i][...].astype(BF16), b[:, c:c + sub], preferred_element_type=F32)
                    for i, b in zip(lhs_of, wb)]
            res = epilogue(accs, ext, c, sub, n)
            for o, r in zip(outs, res):
                if transposed:
                    o[c:c + sub, :] = r.T.astype(o.dtype)
                else:
                    o[:, c:c + sub] = r.astype(o.dtype)

    body(lhs_main, ext_main, out_main, main_transposed)
    if side_fn is not None:
        side_fn(n, m, *side_in, *side_out)
    if has_tail:
        @pl.when(m == mt - 1)
        def _tail():
            body(lhs_tail, ext_tail, out_tail)


def _wres_matmul(pairs, exts, outs, *, n_cols, tn, tm, sub, epilogue, name, side=None,
                 vmem_bytes=VMEM_BUDGET_BYTES, main_transposed=False):
    m_main = pairs[0][0].shape[0]
    has_tail = pairs[0][1] is not None
    mt = m_main // tm
    n_tiles = n_cols // tn
    grid = (n_tiles, mt)

    lhs, lhs_of = [], []
    for lm, lt, _, _ in pairs:
        ids = [i for i, (a, _) in enumerate(lhs) if a is lm]
        if not ids:
            lhs.append((lm, lt))
        lhs_of.append(ids[0] if ids else len(lhs) - 1)

    in_specs, args = [], []
    for lm, _ in lhs:
        in_specs.append(pl.BlockSpec((tm, lm.shape[1]), lambda n, m: (m, 0)))
        args.append(lm)
    if has_tail:
        for _, lt in lhs:
            in_specs.append(pl.BlockSpec(lt.shape, lambda n, m: (0, 0)))
            args.append(lt)
    w_cols = []
    for _, _, w, cols in pairs:
        segs = ((0, cols),) if isinstance(cols, int) else tuple(cols)
        for i, (first_tile, first_col) in enumerate(segs):
            last_tile = segs[i + 1][0] if i + 1 < len(segs) else n_tiles
            assert first_col % tn == 0 and first_col + (last_tile - first_tile) * tn <= w.shape[1]
        w_cols.append(segs)
        in_specs.append(pl.BlockSpec(memory_space=pl.ANY))
        args.append(w)

    def ext_spec(arr, kind, off, tail):
        rows = arr.shape[0] if tail else tm
        if kind == "tile":
            assert off % tn == 0
            if tail:
                return pl.BlockSpec((rows, tn), lambda n, m, o=off // tn: (0, o + n))
            return pl.BlockSpec((rows, tn), lambda n, m, o=off // tn: (m, o + n))
        assert kind == "rope"
        if tail:
            return pl.BlockSpec(arr.shape, lambda n, m: (0, 0))
        per = arr.shape[0] // tm
        return pl.BlockSpec((tm, arr.shape[1]), lambda n, m: (lax.rem(m, per), 0))

    for em, _, kind, off in exts:
        in_specs.append(ext_spec(em, kind, off, False))
        args.append(em)
    if has_tail:
        for _, et, kind, off in exts:
            in_specs.append(ext_spec(et, kind, off, True))
            args.append(et)

    out_specs, out_shape = [], []
    for dm, _ in outs:
        if main_transposed:
            out_specs.append(pl.BlockSpec((tn, tm), lambda n, m: (n, m)))
            out_shape.append(jax.ShapeDtypeStruct((n_cols, m_main), dm))
        else:
            out_specs.append(pl.BlockSpec((tm, tn), lambda n, m: (m, n)))
            out_shape.append(jax.ShapeDtypeStruct((m_main, n_cols), dm))
    if has_tail:
        m_tail = pairs[0][1].shape[0]
        for _, dt in outs:
            out_specs.append(pl.BlockSpec((m_tail, tn), lambda n, m: (0, n)))
            out_shape.append(jax.ShapeDtypeStruct((m_tail, n_cols), dt))

    if side is not None:
        assert n_tiles * mt >= side["n_blocks"]
        step = lambda n, m: jnp.minimum(n * mt + m, side["n_blocks"] - 1)
        in_specs += [pl.BlockSpec(shape, lambda n, m, f=f: f(step(n, m)))
                     for shape, f in side["in_specs"]]
        args += side["args"]
        out_specs += [pl.BlockSpec(shape, lambda n, m, f=f: f(step(n, m)))
                      for shape, f in side["out_specs"]]
        out_shape += side["out_shape"]
        side_fn = lambda n, m, *refs: side["fn"](step(n, m), *refs)
    else:
        side_fn = None

    scratch = ([pltpu.VMEM((w.shape[0], tn), F32) for _, _, w, _ in pairs]
               + [pltpu.VMEM((w.shape[0], tn), BF16) for _, _, w, _ in pairs]
               + [pltpu.SemaphoreType.DMA((len(pairs),))])
    kern = functools.partial(_wres_kernel, lhs_of=tuple(lhs_of), w_cols=tuple(w_cols),
                             n_tiles=n_tiles, n_ext=len(exts), n_out=len(outs),
                             mt=mt, has_tail=has_tail, sub=sub, epilogue=epilogue,
                             side_fn=side_fn,
                             n_side_in=len(side["args"]) if side else 0,
                             n_side_out=len(side["out_shape"]) if side else 0,
                             main_transposed=main_transposed)
    return pl.pallas_call(
        kern, grid=grid, in_specs=in_specs, out_specs=out_specs, out_shape=out_shape,
        scratch_shapes=scratch, compiler_params=_params(("arbitrary", "arbitrary"), vmem_bytes),
        name=name,
    )(*args)


def _side(fn, n_blocks, args, in_specs, out_specs, out_shape):
    return dict(fn=fn, n_blocks=n_blocks, args=args, in_specs=in_specs, out_specs=out_specs,
                out_shape=out_shape)


def _sigmoid(x):
    return 0.5 * jnp.tanh(0.5 * x) + 0.5


def _epi_identity(accs, ext, c, sub, n):
    return [accs[0]]


def _epi_sigmoid(accs, ext, c, sub, n):
    return [_sigmoid(accs[0])]


GELU_C1 = (2.0 / 3.141592653589793) ** 0.5
GELU_C2 = GELU_C1 * 0.044715


def _gelu_tanh_arg(x):
    return x * (GELU_C1 + GELU_C2 * (x * x))


def _epi_gelu_silu(accs, ext, c, sub, n):
    u, g = accs
    return [(0.25 * (u * g)) * ((1.0 + jnp.tanh(_gelu_tanh_arg(u))) * (1.0 + jnp.tanh(0.5 * g)))]


def _epi_gelu(accs, ext, c, sub, n):
    x = accs[0]
    return [(0.5 * x) * (1.0 + jnp.tanh(_gelu_tanh_arg(x)))]


def _epi_silu(accs, ext, c, sub, n):
    return [accs[0] * _sigmoid(accs[0])]


def _epi_rope(scale, accs, ext, c, sub, n):
    assert sub == RET_HEAD_DIM
    x = accs[0]
    x1, x2 = x[:, :ROPE_HALF], x[:, ROPE_HALF:]
    cos, sin = ext[0][...], ext[1][...]
    out = jnp.concatenate([x1 * cos - x2 * sin, x1 * sin + x2 * cos], axis=-1)
    return [out if scale == 1.0 else out * scale]


def _epi_merge(accs, ext, c, sub, n):
    g0, g1, g2 = (e[:, c:c + sub] for e in ext)
    return [g0 * accs[0] + g1 * accs[1] + g2 * accs[2]]


def _epi_residual(accs, ext, c, sub, n):
    return [ext[0][:, c:c + sub] + accs[0]]


SPATIAL_ROWS = 256


def _layer_norm(x, g):
    mu = jnp.mean(x, axis=-1, keepdims=True)
    var = jnp.mean(jnp.square(x - mu), axis=-1, keepdims=True)
    return (x - mu) * lax.rsqrt(var + EPS) * g


def _spatial_main_block(ug_ref, gv_ref, ga_ref, ws_ref, bst_ref, o_ref):
    rows = gv_ref.shape[0]
    gw = A_WIDTH // A_GROUPS
    vn = _layer_norm(gv_ref[...].astype(F32), ga_ref[...]).astype(BF16)
    ri = lax.broadcasted_iota(jnp.int32, (CHUNK, CHUNK), 0)
    ci = lax.broadcasted_iota(jnp.int32, (CHUNK, CHUNK), 1)
    for g in range(A_GROUPS):
        w = jnp.where(ri >= ci, ws_ref[g], 0.0).astype(BF16)
        bias = bst_ref[:, g:g + 1]
        for c in range(rows // CHUNK):
            rs = slice(c * CHUNK, (c + 1) * CHUNK)
            cs = slice(g * gw, (g + 1) * gw)
            sp = jnp.dot(w, vn[rs, cs], preferred_element_type=F32) + bias
            o_ref[rs, cs] = (ug_ref[rs, cs] * sp).astype(o_ref.dtype)


def _spatial_main_side(ug, gv_src, g_anorm, w_s, b_s):
    gv, gv_blk = gv_src
    rows = SPATIAL_ROWS
    return _side(
        lambda blk, *refs: _spatial_main_block(*refs), M_MAIN // rows,
        args=[ug, gv, g_anorm.reshape(1, A_WIDTH), w_s, b_s.T],
        in_specs=[((rows, A_WIDTH), lambda i: (i, 0)),
                  ((rows, A_WIDTH), lambda i: (i, gv_blk)),
                  ((1, A_WIDTH), lambda i: (0, 0)),
                  ((A_GROUPS, CHUNK, CHUNK), lambda i: (0, 0, 0)),
                  ((CHUNK, A_GROUPS), lambda i: (0, 0))],
        out_specs=[((rows, A_WIDTH), lambda i: (i, 0))],
        out_shape=[jax.ShapeDtypeStruct((M_MAIN, A_WIDTH), BF16)])


def _spatial_tail_kernel(ug_ref, gv_ref, ga_ref, ws_ref, bs_ref, o_ref, vn_ref, *, gv_col):
    gw = A_WIDTH // A_GROUPS
    vn = _layer_norm(gv_ref[:, gv_col:gv_col + A_WIDTH], ga_ref[...])
    vn_ref[...] = vn
    for g in range(A_GROUPS):
        cs = slice(g * gw, (g + 1) * gw)
        sp = vn[:, cs] * ws_ref[g, 0:1, 0:1] + bs_ref[g:g + 1, 0:1]
        o_ref[:, cs] = (ug_ref[:, cs] * sp).astype(o_ref.dtype)


def _spatial_tail(ug, gv, gv_col, g_anorm, w_s, b_s):
    m = ug.shape[0]
    assert DEPTH == 1 and DEC_SEQ == 1 and m == DEC_BATCH
    whole = lambda a: pl.BlockSpec(a.shape, lambda i, nd=a.ndim: (0,) * nd)
    args = (ug, gv, g_anorm.reshape(1, A_WIDTH), w_s, b_s)
    return pl.pallas_call(
        functools.partial(_spatial_tail_kernel, gv_col=gv_col),
        grid=(1,),
        in_specs=[whole(a) for a in args],
        out_specs=(pl.BlockSpec((m, A_WIDTH), lambda i: (0, 0)),
                   pl.BlockSpec((None, m, None, A_WIDTH), lambda i: (0, 0, 0, 0))),
        out_shape=(jax.ShapeDtypeStruct((m, A_WIDTH), BF16),
                   jax.ShapeDtypeStruct((DEPTH, DEC_BATCH, DEC_SEQ, A_WIDTH), F32)),
        compiler_params=_params(("arbitrary",)),
        name="spatial_tail",
    )(*args)


def _log_gamma(shape, head):
    return jnp.log1p(-jnp.exp2(jnp.full(shape, -5.0, F32) - head.astype(F32)))


def _group_norm(o, g):
    mu = jnp.mean(o, axis=-1, keepdims=True)
    var = jnp.mean(jnp.square(o - mu), axis=-1, keepdims=True)
    return (o - mu) * lax.rsqrt(var + EPS) * g


def _ret_main_kernel(*refs, side_fn, n_side_in):
    q_ref, kt_ref, v_ref, sg_ref, gr_ref = refs[:5]
    side_in = refs[5:5 + n_side_in]
    o_ref, r_ref = refs[5 + n_side_in:7 + n_side_in]
    side_out = refs[7 + n_side_in:]
    head = pl.program_id(1)
    L, dk = CHUNK, RET_HEAD_DIM
    ri = lax.broadcasted_iota(jnp.int32, (L, L), 0).astype(F32)
    ci = lax.broadcasted_iota(jnp.int32, (L, L), 1).astype(F32)
    diff = ri - ci
    decay_in = jnp.where(diff >= 0, jnp.exp(_log_gamma((L, L), head) * jnp.maximum(diff, 0.0)), 0.0)
    rw = lax.broadcasted_iota(jnp.int32, (L, dk), 0).astype(F32)
    decay_q = jnp.exp(_log_gamma((L, dk), head) * (rw + 1.0))
    cw = lax.broadcasted_iota(jnp.int32, (dk, L), 1).astype(F32)
    decay_kt = jnp.exp(_log_gamma((dk, L), head) * (L - 1.0 - cw))
    decay_blk = jnp.exp(_log_gamma((dk, dk), head) * L)
    gr = gr_ref[...]

    R = jnp.zeros((dk, dk), F32)
    for i in range(SEQ // L):
        rs = slice(i * L, (i + 1) * L)
        qi, kti, vi = q_ref[rs, :], kt_ref[:, rs], v_ref[rs, :]
        s = jnp.dot(qi, kti, preferred_element_type=F32) * decay_in
        o = (jnp.dot(s.astype(BF16), vi, preferred_element_type=F32)
             + jnp.dot(qi, R.astype(BF16), preferred_element_type=F32) * decay_q)
        kdt = (kti.astype(F32) * decay_kt).astype(BF16)
        R = R * decay_blk + jnp.dot(kdt, vi, preferred_element_type=F32)
        o_ref[rs, :] = (_group_norm(o, gr) * sg_ref[rs, :]).astype(o_ref.dtype)
    r_ref[...] = R
    side_fn(pl.program_id(0) * RET_HEADS + head, *side_in, *side_out)


def _ret_main(q, kt, v_src, sg_src, g_ret, side):
    dk = RET_HEAD_DIM
    (v, v_blk), (sg, sg_blk) = v_src, sg_src
    blk = lambda off: pl.BlockSpec((SEQ, dk), lambda b, h: (b, off + h))
    step = lambda b, h: b * RET_HEADS + h
    assert BATCH * RET_HEADS == side["n_blocks"]
    in_specs = [blk(0), pl.BlockSpec((dk, SEQ), lambda b, h: (h, b)), blk(v_blk), blk(sg_blk),
                pl.BlockSpec((1, dk), lambda b, h: (0, h))]
    in_specs += [pl.BlockSpec(shape, lambda b, h, f=f: f(step(b, h))) for shape, f in side["in_specs"]]
    out_specs = [blk(0), pl.BlockSpec((None, None, None, dk, dk), lambda b, h: (0, b, h, 0, 0))]
    out_specs += [pl.BlockSpec(shape, lambda b, h, f=f: f(step(b, h))) for shape, f in side["out_specs"]]
    out_shape = [jax.ShapeDtypeStruct((M_MAIN, RET_WIDTH), BF16),
                 jax.ShapeDtypeStruct((DEPTH, BATCH, RET_HEADS, dk, dk), F32)] + side["out_shape"]
    kern = functools.partial(_ret_main_kernel, side_fn=side["fn"], n_side_in=len(side["args"]))
    return pl.pallas_call(
        kern, grid=(BATCH, RET_HEADS), in_specs=in_specs, out_specs=out_specs, out_shape=out_shape,
        compiler_params=_params(("arbitrary", "arbitrary")), name="ret_main",
    )(q, kt, v, sg, g_ret.reshape(1, RET_WIDTH), *side["args"])


RET_TAIL_ROWS = 16


def _ret_tail_block(blk, q_ref, k_ref, v_ref, sg_ref, gr_ref, s_ref, o_ref, so_ref):
    head = blk % RET_HEADS
    nb, dk = q_ref.shape
    gam_row = jnp.exp(_log_gamma((nb, dk), head))
    gam_st = jnp.exp(_log_gamma((dk, dk), head))
    q, k, v = q_ref[...], k_ref[...], v_ref[...]
    qt, kt = q.T, k.T
    rows = []
    for r in range(nb):
        R = s_ref[r]
        so_ref[r] = R * gam_st + kt[:, r:r + 1] * v[r:r + 1, :]
        rows.append(jnp.sum(qt[:, r:r + 1] * R, axis=0, keepdims=True))
    qr = jnp.concatenate(rows, axis=0)
    qk = jnp.sum(q * k, axis=-1, keepdims=True)
    o = qk * v + qr * gam_row
    o_ref[...] = _group_norm(o, gr_ref[...]) * sg_ref[...]


def _ret_tail_side(q, k, v_src, sg_src, g_ret, state):
    dk, nb = RET_HEAD_DIM, RET_TAIL_ROWS
    (v, v_blk), (sg, sg_blk) = v_src, sg_src
    row = lambda off: ((nb, dk), lambda i: (i // RET_HEADS, off + i % RET_HEADS))
    st = ((None, nb, None, dk, dk), lambda i: (0, i // RET_HEADS, i % RET_HEADS, 0, 0))
    return _side(
        _ret_tail_block, (DEC_BATCH // nb) * RET_HEADS,
        args=[q, k, v, sg, g_ret.reshape(1, RET_WIDTH), state],
        in_specs=[row(0), row(0), row(v_blk), row(sg_blk),
                  ((1, dk), lambda i: (0, i % RET_HEADS)), st],
        out_specs=[row(0), st],
        out_shape=[jax.ShapeDtypeStruct((M_TAIL, RET_WIDTH), F32),
                   jax.ShapeDtypeStruct(state.shape, F32)])


XATTN_ROWS = 512


def _xattn_main_block(q_ref, k_ref, v_ref, g_ref, o_ref):
    k = k_ref[...].astype(BF16)
    v = v_ref[...].astype(BF16)
    for t in range(q_ref.shape[0] // XATTN_ROWS):
        rs = slice(t * XATTN_ROWS, (t + 1) * XATTN_ROWS)
        sc = lax.dot_general(q_ref[rs, :], k, (((1,), (1,)), ((), ())), preferred_element_type=F32)
        sc = sc * (MEM_HEAD_DIM ** -0.5)
        e = jnp.exp(sc - jnp.max(sc, axis=-1, keepdims=True))
        p = e * (1.0 / jnp.sum(e, axis=-1, keepdims=True))
        om = jnp.dot(p.astype(BF16), v, preferred_element_type=F32)
        o_ref[rs, :] = (om * g_ref[rs, :]).astype(o_ref.dtype)


def _xattn_main_side(cq_src, scg_src, kv):
    dh = MEM_HEAD_DIM
    (cq, cq_blk), (scg, scg_blk) = cq_src, scg_src
    qspec = lambda off: ((SEQ, dh), lambda i: (i // MEM_HEADS, off + i % MEM_HEADS))
    kvspec = lambda off: ((MEM_LEN, dh), lambda i: (i // MEM_HEADS, off + i % MEM_HEADS))
    return _side(
        lambda blk, *refs: _xattn_main_block(*refs), BATCH * MEM_HEADS,
        args=[cq, kv, kv, scg],
        in_specs=[qspec(cq_blk), kvspec(0), kvspec(MEM_HEADS), qspec(scg_blk)],
        out_specs=[qspec(0)],
        out_shape=[jax.ShapeDtypeStruct((M_MAIN, MEM_WIDTH), BF16)])


XATTN_TAIL_ROWS = 4


def _xattn_tail_block(q_ref, g_ref, k_ref, v_ref, o_ref):
    for r in range(q_ref.shape[0]):
        q = q_ref[r] * (MEM_HEAD_DIM ** -0.5)
        sc = jnp.sum(k_ref[r] * q[None], axis=-1, keepdims=True)
        e = jnp.exp(sc - jnp.max(sc, axis=0, keepdims=True))
        o_ref[r] = jnp.sum(e * v_ref[r], axis=0) / jnp.sum(e, axis=0) * g_ref[r]


def _xattn_tail_side(cq, scg, ck, cv):
    nb = XATTN_TAIL_ROWS
    shape3 = (M_TAIL, MEM_HEADS, MEM_HEAD_DIM)
    row = ((nb, MEM_HEADS, MEM_HEAD_DIM), lambda i: (i, 0, 0))
    kv = ((None, nb, MEM_LEN, MEM_HEADS, MEM_HEAD_DIM), lambda i: (0, i, 0, 0, 0))
    return _side(
        lambda blk, *refs: _xattn_tail_block(*refs), DEC_BATCH // nb,
        args=[cq.reshape(shape3), scg.reshape(shape3), ck, cv],
        in_specs=[row, row, kv, kv], out_specs=[row],
        out_shape=[jax.ShapeDtypeStruct(shape3, F32)])


def kernel(x_prompt, x_sample, state_ret, cache_mem_k, cache_mem_v, mem_prompt, g_pre, w_in, g_anorm,
           w_s, b_s, g_ret, g_mem, w_mem_kv, w_out_a, w_out_b, w_out_c, w_out, g_final):
    assert DEPTH == 1 and w_in.shape == (DEPTH, D_MODEL, IN_WIDTH)
    xp = x_prompt.reshape(M_MAIN, D_MODEL)
    xs = x_sample.reshape(M_TAIL, D_MODEL)
    win = w_in.reshape(D_MODEL, IN_WIDTH)

    h_m = _rmsnorm(xp, g_pre[0], 512, BF16)
    h_t = _rmsnorm(xs, g_pre[0], M_TAIL, BF16)
    cos_m, sin_m, cos_t, sin_t = _rope_tables()

    hm = _rmsnorm(mem_prompt.reshape(BATCH * MEM_LEN, D_MODEL), g_mem[0], 512, BF16)
    wkv = w_mem_kv.reshape(D_MODEL, 2 * MEM_WIDTH)
    (kv,) = _wres_matmul([(hm, None, wkv, 0)], [], [(F32, None)], n_cols=2 * MEM_WIDTH, tn=1024,
                         tm=512, sub=256, epilogue=_epi_identity, name="mem_kv")

    tn = tm = 1024

    def inproj(cols, n_cols, epilogue, out_dtypes, name, tn=tn, exts=(), **kw):
        return _wres_matmul([(h_m, h_t, win, c) for c in cols], list(exts), [out_dtypes],
                            n_cols=n_cols, tn=tn, tm=tm, sub=256, epilogue=epilogue, name=name, **kw)

    second = RET_WIDTH // tn
    id_m, id_t = inproj([((0, OFF_RV), (second, OFF_CQ))], RET_WIDTH + MEM_WIDTH, _epi_identity,
                        (BF16, F32), "inproj_v_cq")
    silu_m, silu_t = inproj([((0, OFF_RG), (second, OFF_CG))], RET_WIDTH + MEM_WIDTH, _epi_silu,
                            (BF16, F32), "inproj_rg_cg")
    gv_m, gv_t = inproj([OFF_AV], A_WIDTH, _epi_gelu, (BF16, F32), "inproj_gv")
    ug_m, ug_t = inproj([OFF_AU, OFF_AG], A_WIDTH, _epi_gelu_silu, (BF16, F32), "inproj_ug", tn=512)

    rope = [(cos_m, cos_t, "rope", 0), (sin_m, sin_t, "rope", 0)]
    q_m, q_t, a_m = inproj(
        [OFF_RQ], RET_WIDTH, functools.partial(_epi_rope, 1.0), (BF16, F32), "inproj_q", exts=rope,
        tn=512, side=_spatial_main_side(ug_m, (gv_m, 0), g_anorm[0], w_s[0], b_s[0]))
    a_t, vn_t = _spatial_tail(ug_t, gv_t, 0, g_anorm[0], w_s[0], b_s[0])
    mem_blk = RET_WIDTH // MEM_HEAD_DIM
    kt_m, k_t, c_m = inproj(
        [OFF_RK], RET_WIDTH, functools.partial(_epi_rope, RET_HEAD_DIM ** -0.5), (BF16, F32),
        "inproj_k", exts=rope, main_transposed=True,
        side=_xattn_main_side((id_m, mem_blk), (silu_m, mem_blk), kv),
        vmem_bytes=VMEM_HOST_BYTES)
    gs_m, gs_t, b_t, ret_s = inproj(
        [OFF_GATES], N_BRANCH * D_MODEL, _epi_sigmoid, (BF16, BF16), "inproj_gates",
        side=_ret_tail_side(q_t, k_t, (id_t, 0), (silu_t, 0), g_ret[0], state_ret),
        vmem_bytes=VMEM_HOST_BYTES)

    b_m, ret_p, c_t3 = _ret_main(
        q_m, kt_m, (id_m, 0), (silu_m, 0), g_ret[0],
        side=_xattn_tail_side(id_t[:, RET_WIDTH:], silu_t[:, RET_WIDTH:], cache_mem_k, cache_mem_v))
    c_t = c_t3.reshape(M_TAIL, MEM_WIDTH)

    merged_m, merged_t = _wres_matmul(
        [(a_m, a_t, w_out_a.reshape(A_WIDTH, D_MODEL), 0),
         (b_m, b_t, w_out_b.reshape(RET_WIDTH, D_MODEL), 0),
         (c_m, c_t, w_out_c.reshape(MEM_WIDTH, D_MODEL), 0)],
        [(gs_m, gs_t, "tile", b * D_MODEL) for b in range(N_BRANCH)],
        [(BF16, BF16)], n_cols=D_MODEL, tn=512, tm=1024, sub=256, epilogue=_epi_merge, name="merge")
    y_m, y_t = _wres_matmul(
        [(merged_m, merged_t, w_out.reshape(D_MODEL, D_MODEL), 0)],
        [(xp, xs, "tile", 0)], [(F32, F32)], n_cols=D_MODEL, tn=1024, tm=1024, sub=256,
        epilogue=_epi_residual, name="outproj", vmem_bytes=VMEM_HOST_BYTES)
    y_prompt = _rmsnorm(y_m, g_final, 512, F32).reshape(BATCH, SEQ, D_MODEL)
    assert DEC_SEQ == 1
    y_sample = _rmsnorm(y_t, g_final, M_TAIL, F32, out_3d=True)

    mem_shape = (DEPTH, BATCH, MEM_LEN, MEM_HEADS, MEM_HEAD_DIM)
    return (y_prompt, y_sample, ret_p,
            kv[:, :MEM_WIDTH].reshape(mem_shape), kv[:, MEM_WIDTH:].reshape(mem_shape),
            ret_s, vn_t)
```

```python
import functools

import jax
import jax.numpy as jnp
from jax import lax
from jax.experimental import pallas as pl
from jax.experimental.pallas import tpu as pltpu

D_MODEL = 4096
BATCH = 4
SEQ = 2048
DEPTH = 1
DEC_BATCH = 128
DEC_SEQ = 1
PAST_LEN = 16384

CHUNK = 128
A_WIDTH = 2048
A_GROUPS = 4
RET_HEADS = 8
RET_HEAD_DIM = 256
RET_WIDTH = RET_HEADS * RET_HEAD_DIM
MEM_LEN = 256
MEM_HEADS = 4
MEM_HEAD_DIM = 256
MEM_WIDTH = MEM_HEADS * MEM_HEAD_DIM
N_BRANCH = 3
ROPE_BASE = 10000.0
EPS = 1e-6

OFF_AU = 0
OFF_AV = OFF_AU + A_WIDTH
OFF_AG = OFF_AV + A_WIDTH
OFF_RQ = OFF_AG + A_WIDTH
OFF_RK = OFF_RQ + RET_WIDTH
OFF_RV = OFF_RK + RET_WIDTH
OFF_RG = OFF_RV + RET_WIDTH
OFF_CQ = OFF_RG + RET_WIDTH
OFF_CG = OFF_CQ + MEM_WIDTH
OFF_GATES = OFF_CG + MEM_WIDTH
IN_WIDTH = OFF_GATES + N_BRANCH * D_MODEL

M_MAIN = BATCH * SEQ
M_TAIL = DEC_BATCH * DEC_SEQ
ROPE_HALF = RET_HEAD_DIM // 2

F32 = jnp.float32
BF16 = jnp.bfloat16

MIB = 1024 * 1024
VMEM_BUDGET_BYTES = 56 * MIB
VMEM_HOST_BYTES = 63 * MIB


def _params(semantics, vmem_bytes=VMEM_BUDGET_BYTES):
    return pltpu.CompilerParams(dimension_semantics=semantics, vmem_limit_bytes=vmem_bytes)


def _rmsnorm_kernel(x_ref, g_ref, o_ref):
    x = x_ref[...].astype(F32)
    y = x * lax.rsqrt(jnp.mean(x * x, axis=-1, keepdims=True) + EPS)
    o_ref[...] = (y * g_ref[...].astype(F32)).astype(o_ref.dtype)


def _rmsnorm(x, g, rows, out_dtype, out_3d=False):
    m, d = x.shape[0], x.shape[-1]
    spec3 = pl.BlockSpec((rows, None, d), lambda i: (i, 0, 0))
    spec2 = pl.BlockSpec((rows, d), lambda i: (i, 0))
    return pl.pallas_call(
        _rmsnorm_kernel,
        grid=(m // rows,),
        in_specs=[spec3 if x.ndim == 3 else spec2, pl.BlockSpec((1, d), lambda i: (0, 0))],
        out_specs=spec3 if out_3d else spec2,
        out_shape=jax.ShapeDtypeStruct((m, 1, d) if out_3d else (m, d), out_dtype),
        compiler_params=_params(("arbitrary",)),
        name="rmsnorm",
    )(x, g.reshape(1, d))


def _rope_kernel(cm_ref, sm_ref, ct_ref, st_ref):
    def table(rows, pos):
        j = lax.broadcasted_iota(jnp.int32, (rows, ROPE_HALF), 1).astype(F32)
        inv = ROPE_BASE ** (-j / ROPE_HALF)
        return pos.astype(F32) * inv

    ang = table(SEQ, lax.broadcasted_iota(jnp.int32, (SEQ, ROPE_HALF), 0))
    cm_ref[...] = jnp.cos(ang)
    sm_ref[...] = jnp.sin(ang)
    r = lax.broadcasted_iota(jnp.int32, (M_TAIL, ROPE_HALF), 0)
    t = jnp.zeros_like(r) if DEC_SEQ == 1 else lax.rem(r, DEC_SEQ)
    ang_t = table(M_TAIL, PAST_LEN + t)
    ct_ref[...] = jnp.cos(ang_t)
    st_ref[...] = jnp.sin(ang_t)


def _rope_tables():
    return pl.pallas_call(
        _rope_kernel,
        out_shape=(jax.ShapeDtypeStruct((SEQ, ROPE_HALF), F32),
                   jax.ShapeDtypeStruct((SEQ, ROPE_HALF), F32),
                   jax.ShapeDtypeStruct((M_TAIL, ROPE_HALF), F32),
                   jax.ShapeDtypeStruct((M_TAIL, ROPE_HALF), F32)),
        name="rope_tables",
    )()


CAST_ROWS = 256


def _wres_kernel(*refs, lhs_of, w_cols, n_tiles, n_ext, n_out, mt, has_tail, sub, epilogue,
                 side_fn, n_side_in, n_side_out, main_transposed):
    refs = list(refs)
    n_pairs, n_lhs = len(lhs_of), max(lhs_of) + 1

    def take(k):
        out = refs[:k]
        del refs[:k]
        return out

    lhs_main = take(n_lhs)
    lhs_tail = take(n_lhs) if has_tail else []
    w_hbm = take(n_pairs)
    ext_main = take(n_ext)
    ext_tail = take(n_ext) if has_tail else []
    side_in = take(n_side_in)
    out_main = take(n_out)
    out_tail = take(n_out) if has_tail else []
    side_out = take(n_side_out)
    wf = take(n_pairs)
    wb = take(n_pairs)
    (sem,) = take(1)

    n = pl.program_id(0)
    m = pl.program_id(1)
    tn = wb[0].shape[1]

    def w_copy(p, tile):
        col = w_cols[p][0][1] + tile * tn
        for first_tile, first_col in w_cols[p][1:]:
            col = jnp.where(tile >= first_tile, first_col + (tile - first_tile) * tn, col)
        col = pl.multiple_of(col, tn)
        return pltpu.make_async_copy(w_hbm[p].at[:, pl.ds(col, tn)], wf[p], sem.at[p])

    @pl.when(m == 0)
    def _weights():
        @pl.when(n == 0)
        def _first():
            for p in range(n_pairs):
                w_copy(p, 0).start()

        for p in range(n_pairs):
            w_copy(p, n).wait()

            def body(i, carry, wf_ref=wf[p], wb_ref=wb[p]):
                r = pl.multiple_of(i * CAST_ROWS, CAST_ROWS)
                wb_ref[pl.ds(r, CAST_ROWS), :] = wf_ref[pl.ds(r, CAST_ROWS), :].astype(BF16)
                return carry
            lax.fori_loop(0, wf[p].shape[0] // CAST_ROWS, body, 0)

        @pl.when(n + 1 < n_tiles)
        def _next():
            for p in range(n_pairs):
                w_copy(p, n + 1).start()

    def body(lhs, ext, outs, transposed=False):
        for c in range(0, tn, sub):
            accs = [jnp.dot(lhs[i][...].astype(BF16), b[:, c:c + sub], preferred_element_type=F32)
                    for i, b in zip(lhs_of, wb)]
            res = epilogue(accs, ext, c, sub, n)
            for o, r in zip(outs, res):
                if transposed:
                    o[c:c + sub, :] = r.T.astype(o.dtype)
                else:
                    o[:, c:c + sub] = r.astype(o.dtype)

    body(lhs_main, ext_main, out_main, main_transposed)
    if side_fn is not None:
        side_fn(n, m, *side_in, *side_out)
    if has_tail:
        @pl.when(m == mt - 1)
        def _tail():
            body(lhs_tail, ext_tail, out_tail)


def _wres_matmul(pairs, exts, outs, *, n_cols, tn, tm, sub, epilogue, name, side=None,
                 vmem_bytes=VMEM_BUDGET_BYTES, main_transposed=False):
    m_main = pairs[0][0].shape[0]
    has_tail = pairs[0][1] is not None
    mt = m_main // tm
    n_tiles = n_cols // tn
    grid = (n_tiles, mt)

    lhs, lhs_of = [], []
    for lm, lt, _, _ in pairs:
        ids = [i for i, (a, _) in enumerate(lhs) if a is lm]
        if not ids:
            lhs.append((lm, lt))
        lhs_of.append(ids[0] if ids else len(lhs) - 1)

    in_specs, args = [], []
    for lm, _ in lhs:
        in_specs.append(pl.BlockSpec((tm, lm.shape[1]), lambda n, m: (m, 0)))
        args.append(lm)
    if has_tail:
        for _, lt in lhs:
            in_specs.append(pl.BlockSpec(lt.shape, lambda n, m: (0, 0)))
            args.append(lt)
    w_cols = []
    for _, _, w, cols in pairs:
        segs = ((0, cols),) if isinstance(cols, int) else tuple(cols)
        for i, (first_tile, first_col) in enumerate(segs):
            last_tile = segs[i + 1][0] if i + 1 < len(segs) else n_tiles
            assert first_col % tn == 0 and first_col + (last_tile - first_tile) * tn <= w.shape[1]
        w_cols.append(segs)
        in_specs.append(pl.BlockSpec(memory_space=pl.ANY))
        args.append(w)

    def ext_spec(arr, kind, off, tail):
        rows = arr.shape[0] if tail else tm
        if kind == "tile":
            assert off % tn == 0
            if tail and arr.ndim == 3:
                return pl.BlockSpec((rows, None, tn), lambda n, m, o=off // tn: (0, 0, o + n))
            if tail:
                return pl.BlockSpec((rows, tn), lambda n, m, o=off // tn: (0, o + n))
            return pl.BlockSpec((rows, tn), lambda n, m, o=off // tn: (m, o + n))
        assert kind == "rope"
        if tail:
            return pl.BlockSpec(arr.shape, lambda n, m: (0, 0))
        per = arr.shape[0] // tm
        return pl.BlockSpec((tm, arr.shape[1]), lambda n, m: (lax.rem(m, per), 0))

    for em, _, kind, off in exts:
        in_specs.append(ext_spec(em, kind, off, False))
        args.append(em)
    if has_tail:
        for _, et, kind, off in exts:
            in_specs.append(ext_spec(et, kind, off, True))
            args.append(et)

    out_specs, out_shape = [], []
    for dm, _ in outs:
        if main_transposed:
            out_specs.append(pl.BlockSpec((tn, tm), lambda n, m: (n, m)))
            out_shape.append(jax.ShapeDtypeStruct((n_cols, m_main), dm))
        else:
            out_specs.append(pl.BlockSpec((tm, tn), lambda n, m: (m, n)))
            out_shape.append(jax.ShapeDtypeStruct((m_main, n_cols), dm))
    if has_tail:
        m_tail = pairs[0][1].shape[0]
        for _, dt in outs:
            out_specs.append(pl.BlockSpec((m_tail, tn), lambda n, m: (0, n)))
            out_shape.append(jax.ShapeDtypeStruct((m_tail, n_cols), dt))

    if side is not None:
        assert n_tiles * mt >= side["n_blocks"]
        step = lambda n, m: jnp.minimum(n * mt + m, side["n_blocks"] - 1)
        in_specs += [pl.BlockSpec(shape, lambda n, m, f=f: f(step(n, m)))
                     for shape, f in side["in_specs"]]
        args += side["args"]
        out_specs += [pl.BlockSpec(shape, lambda n, m, f=f: f(step(n, m)))
                      for shape, f in side["out_specs"]]
        out_shape += side["out_shape"]
        side_fn = lambda n, m, *refs: side["fn"](step(n, m), *refs)
    else:
        side_fn = None

    scratch = ([pltpu.VMEM((w.shape[0], tn), F32) for _, _, w, _ in pairs]
               + [pltpu.VMEM((w.shape[0], tn), BF16) for _, _, w, _ in pairs]
               + [pltpu.SemaphoreType.DMA((len(pairs),))])
    kern = functools.partial(_wres_kernel, lhs_of=tuple(lhs_of), w_cols=tuple(w_cols),
                             n_tiles=n_tiles, n_ext=len(exts), n_out=len(outs),
                             mt=mt, has_tail=has_tail, sub=sub, epilogue=epilogue,
                             side_fn=side_fn,
                             n_side_in=len(side["args"]) if side else 0,
                             n_side_out=len(side["out_shape"]) if side else 0,
                             main_transposed=main_transposed)
    return pl.pallas_call(
        kern, grid=grid, in_specs=in_specs, out_specs=out_specs, out_shape=out_shape,
        scratch_shapes=scratch, compiler_params=_params(("arbitrary", "arbitrary"), vmem_bytes),
        name=name,
    )(*args)


def _side(fn, n_blocks, args, in_specs, out_specs, out_shape):
    return dict(fn=fn, n_blocks=n_blocks, args=args, in_specs=in_specs, out_specs=out_specs,
                out_shape=out_shape)


def _sigmoid(x):
    return 0.5 * jnp.tanh(0.5 * x) + 0.5


def _epi_identity(accs, ext, c, sub, n):
    return [accs[0]]


def _epi_sigmoid(accs, ext, c, sub, n):
    return [_sigmoid(accs[0])]


GELU_C1 = (2.0 / 3.141592653589793) ** 0.5
GELU_C2 = GELU_C1 * 0.044715


def _gelu_tanh_arg(x):
    return x * (GELU_C1 + GELU_C2 * (x * x))


def _epi_gelu_silu(accs, ext, c, sub, n):
    u, g = accs
    return [(0.25 * (u * g)) * ((1.0 + jnp.tanh(_gelu_tanh_arg(u))) * (1.0 + jnp.tanh(0.5 * g)))]


def _epi_gelu(accs, ext, c, sub, n):
    x = accs[0]
    return [(0.5 * x) * (1.0 + jnp.tanh(_gelu_tanh_arg(x)))]


def _epi_silu(accs, ext, c, sub, n):
    return [accs[0] * _sigmoid(accs[0])]


def _epi_rope(scale, accs, ext, c, sub, n):
    assert sub == RET_HEAD_DIM
    x = accs[0]
    x1, x2 = x[:, :ROPE_HALF], x[:, ROPE_HALF:]
    cos, sin = ext[0][...], ext[1][...]
    out = jnp.concatenate([x1 * cos - x2 * sin, x1 * sin + x2 * cos], axis=-1)
    return [out if scale == 1.0 else out * scale]


def _epi_merge(accs, ext, c, sub, n):
    g0, g1, g2 = (e[:, c:c + sub] for e in ext)
    return [g0 * accs[0] + g1 * accs[1] + g2 * accs[2]]


def _epi_residual(accs, ext, c, sub, n):
    return [ext[0][:, c:c + sub] + accs[0]]


SPATIAL_ROWS = 256


def _layer_norm(x, g):
    mu = jnp.mean(x, axis=-1, keepdims=True)
    var = jnp.mean(jnp.square(x - mu), axis=-1, keepdims=True)
    return (x - mu) * lax.rsqrt(var + EPS) * g


def _spatial_main_block(ug_ref, gv_ref, ga_ref, ws_ref, bst_ref, o_ref):
    rows = gv_ref.shape[0]
    gw = A_WIDTH // A_GROUPS
    vn = _layer_norm(gv_ref[...].astype(F32), ga_ref[...]).astype(BF16)
    ri = lax.broadcasted_iota(jnp.int32, (CHUNK, CHUNK), 0)
    ci = lax.broadcasted_iota(jnp.int32, (CHUNK, CHUNK), 1)
    for g in range(A_GROUPS):
        w = jnp.where(ri >= ci, ws_ref[g], 0.0).astype(BF16)
        bias = bst_ref[:, g:g + 1]
        for c in range(rows // CHUNK):
            rs = slice(c * CHUNK, (c + 1) * CHUNK)
            cs = slice(g * gw, (g + 1) * gw)
            sp = jnp.dot(w, vn[rs, cs], preferred_element_type=F32) + bias
            o_ref[rs, cs] = (ug_ref[rs, cs] * sp).astype(o_ref.dtype)


def _spatial_main_side(ug, gv_src, g_anorm, w_s, b_s):
    gv, gv_blk = gv_src
    rows = SPATIAL_ROWS
    return _side(
        lambda blk, *refs: _spatial_main_block(*refs), M_MAIN // rows,
        args=[ug, gv, g_anorm.reshape(1, A_WIDTH), w_s, b_s.T],
        in_specs=[((rows, A_WIDTH), lambda i: (i, 0)),
                  ((rows, A_WIDTH), lambda i: (i, gv_blk)),
                  ((1, A_WIDTH), lambda i: (0, 0)),
                  ((A_GROUPS, CHUNK, CHUNK), lambda i: (0, 0, 0)),
                  ((CHUNK, A_GROUPS), lambda i: (0, 0))],
        out_specs=[((rows, A_WIDTH), lambda i: (i, 0))],
        out_shape=[jax.ShapeDtypeStruct((M_MAIN, A_WIDTH), BF16)])


def _spatial_tail_kernel(ug_ref, gv_ref, ga_ref, ws_ref, bs_ref, o_ref, vn_ref, *, gv_col):
    gw = A_WIDTH // A_GROUPS
    vn = _layer_norm(gv_ref[:, gv_col:gv_col + A_WIDTH], ga_ref[...])
    vn_ref[...] = vn
    for g in range(A_GROUPS):
        cs = slice(g * gw, (g + 1) * gw)
        sp = vn[:, cs] * ws_ref[g, 0:1, 0:1] + bs_ref[g:g + 1, 0:1]
        o_ref[:, cs] = (ug_ref[:, cs] * sp).astype(o_ref.dtype)


def _spatial_tail(ug, gv, gv_col, g_anorm, w_s, b_s):
    m = ug.shape[0]
    assert DEPTH == 1 and DEC_SEQ == 1 and m == DEC_BATCH
    whole = lambda a: pl.BlockSpec(a.shape, lambda i, nd=a.ndim: (0,) * nd)
    args = (ug, gv, g_anorm.reshape(1, A_WIDTH), w_s, b_s)
    return pl.pallas_call(
        functools.partial(_spatial_tail_kernel, gv_col=gv_col),
        grid=(1,),
        in_specs=[whole(a) for a in args],
        out_specs=(pl.BlockSpec((m, A_WIDTH), lambda i: (0, 0)),
                   pl.BlockSpec((None, m, None, A_WIDTH), lambda i: (0, 0, 0, 0))),
        out_shape=(jax.ShapeDtypeStruct((m, A_WIDTH), BF16),
                   jax.ShapeDtypeStruct((DEPTH, DEC_BATCH, DEC_SEQ, A_WIDTH), F32)),
        compiler_params=_params(("arbitrary",)),
        name="spatial_tail",
    )(*args)


def _log_gamma(shape, head):
    return jnp.log1p(-jnp.exp2(jnp.full(shape, -5.0, F32) - head.astype(F32)))


def _group_norm(o, g):
    mu = jnp.mean(o, axis=-1, keepdims=True)
    var = jnp.mean(jnp.square(o - mu), axis=-1, keepdims=True)
    return (o - mu) * lax.rsqrt(var + EPS) * g


def _ret_main_kernel(*refs, side_fn, n_side_in):
    q_ref, kt_ref, v_ref, sg_ref, gr_ref = refs[:5]
    side_in = refs[5:5 + n_side_in]
    o_ref, r_ref = refs[5 + n_side_in:7 + n_side_in]
    side_out = refs[7 + n_side_in:]
    head = pl.program_id(1)
    L, dk = CHUNK, RET_HEAD_DIM
    ri = lax.broadcasted_iota(jnp.int32, (L, L), 0).astype(F32)
    ci = lax.broadcasted_iota(jnp.int32, (L, L), 1).astype(F32)
    diff = ri - ci
    decay_in = jnp.where(diff >= 0, jnp.exp(_log_gamma((L, L), head) * jnp.maximum(diff, 0.0)), 0.0)
    rw = lax.broadcasted_iota(jnp.int32, (L, dk), 0).astype(F32)
    decay_q = jnp.exp(_log_gamma((L, dk), head) * (rw + 1.0))
    cw = lax.broadcasted_iota(jnp.int32, (dk, L), 1).astype(F32)
    decay_kt = jnp.exp(_log_gamma((dk, L), head) * (L - 1.0 - cw))
    decay_blk = jnp.exp(_log_gamma((dk, dk), head) * L)
    gr = gr_ref[...]

    R = jnp.zeros((dk, dk), F32)
    for i in range(SEQ // L):
        rs = slice(i * L, (i + 1) * L)
        qi, kti, vi = q_ref[rs, :], kt_ref[:, rs], v_ref[rs, :]
        s = jnp.dot(qi, kti, preferred_element_type=F32) * decay_in
        o = (jnp.dot(s.astype(BF16), vi, preferred_element_type=F32)
             + jnp.dot(qi, R.astype(BF16), preferred_element_type=F32) * decay_q)
        kdt = (kti.astype(F32) * decay_kt).astype(BF16)
        R = R * decay_blk + jnp.dot(kdt, vi, preferred_element_type=F32)
        o_ref[rs, :] = (_group_norm(o, gr) * sg_ref[rs, :]).astype(o_ref.dtype)
    r_ref[...] = R
    side_fn(pl.program_id(0) * RET_HEADS + head, *side_in, *side_out)


def _ret_main(q, kt, v_src, sg_src, g_ret, side):
    dk = RET_HEAD_DIM
    (v, v_blk), (sg, sg_blk) = v_src, sg_src
    blk = lambda off: pl.BlockSpec((SEQ, dk), lambda b, h: (b, off + h))
    step = lambda b, h: b * RET_HEADS + h
    assert BATCH * RET_HEADS == side["n_blocks"]
    in_specs = [blk(0), pl.BlockSpec((dk, SEQ), lambda b, h: (h, b)), blk(v_blk), blk(sg_blk),
                pl.BlockSpec((1, dk), lambda b, h: (0, h))]
    in_specs += [pl.BlockSpec(shape, lambda b, h, f=f: f(step(b, h))) for shape, f in side["in_specs"]]
    out_specs = [blk(0), pl.BlockSpec((None, None, None, dk, dk), lambda b, h: (0, b, h, 0, 0))]
    out_specs += [pl.BlockSpec(shape, lambda b, h, f=f: f(step(b, h))) for shape, f in side["out_specs"]]
    out_shape = [jax.ShapeDtypeStruct((M_MAIN, RET_WIDTH), BF16),
                 jax.ShapeDtypeStruct((DEPTH, BATCH, RET_HEADS, dk, dk), F32)] + side["out_shape"]
    kern = functools.partial(_ret_main_kernel, side_fn=side["fn"], n_side_in=len(side["args"]))
    return pl.pallas_call(
        kern, grid=(BATCH, RET_HEADS), in_specs=in_specs, out_specs=out_specs, out_shape=out_shape,
        compiler_params=_params(("arbitrary", "arbitrary")), name="ret_main",
    )(q, kt, v, sg, g_ret.reshape(1, RET_WIDTH), *side["args"])


RET_TAIL_ROWS = 16


def _ret_tail_block(blk, q_ref, k_ref, v_ref, sg_ref, gr_ref, s_ref, o_ref, so_ref):
    head = blk % RET_HEADS
    nb, dk = q_ref.shape
    gam_row = jnp.exp(_log_gamma((nb, dk), head))
    gam_st = jnp.exp(_log_gamma((dk, dk), head))
    q, k, v = q_ref[...], k_ref[...], v_ref[...]
    qt, kt = q.T, k.T
    rows = []
    for r in range(nb):
        R = s_ref[r]
        so_ref[r] = R * gam_st + kt[:, r:r + 1] * v[r:r + 1, :]
        rows.append(jnp.sum(qt[:, r:r + 1] * R, axis=0, keepdims=True))
    qr = jnp.concatenate(rows, axis=0)
    qk = jnp.sum(q * k, axis=-1, keepdims=True)
    o = qk * v + qr * gam_row
    o_ref[...] = _group_norm(o, gr_ref[...]) * sg_ref[...]


def _ret_tail_side(q, k, v_src, sg_src, g_ret, state):
    dk, nb = RET_HEAD_DIM, RET_TAIL_ROWS
    (v, v_blk), (sg, sg_blk) = v_src, sg_src
    row = lambda off: ((nb, dk), lambda i: (i // RET_HEADS, off + i % RET_HEADS))
    st = ((None, nb, None, dk, dk), lambda i: (0, i // RET_HEADS, i % RET_HEADS, 0, 0))
    return _side(
        _ret_tail_block, (DEC_BATCH // nb) * RET_HEADS,
        args=[q, k, v, sg, g_ret.reshape(1, RET_WIDTH), state],
        in_specs=[row(0), row(0), row(v_blk), row(sg_blk),
                  ((1, dk), lambda i: (0, i % RET_HEADS)), st],
        out_specs=[row(0), st],
        out_shape=[jax.ShapeDtypeStruct((M_TAIL, RET_WIDTH), F32),
                   jax.ShapeDtypeStruct(state.shape, F32)])


XATTN_ROWS = 512


def _xattn_main_block(q_ref, k_ref, v_ref, g_ref, o_ref):
    k = k_ref[...].astype(BF16)
    v = v_ref[...].astype(BF16)
    for t in range(q_ref.shape[0] // XATTN_ROWS):
        rs = slice(t * XATTN_ROWS, (t + 1) * XATTN_ROWS)
        sc = lax.dot_general(q_ref[rs, :], k, (((1,), (1,)), ((), ())), preferred_element_type=F32)
        sc = sc * (MEM_HEAD_DIM ** -0.5)
        e = jnp.exp(sc - jnp.max(sc, axis=-1, keepdims=True))
        p = e * (1.0 / jnp.sum(e, axis=-1, keepdims=True))
        om = jnp.dot(p.astype(BF16), v, preferred_element_type=F32)
        o_ref[rs, :] = (om * g_ref[rs, :]).astype(o_ref.dtype)


def _xattn_main_side(cq_src, scg_src, kv):
    dh = MEM_HEAD_DIM
    (cq, cq_blk), (scg, scg_blk) = cq_src, scg_src
    qspec = lambda off: ((SEQ, dh), lambda i: (i // MEM_HEADS, off + i % MEM_HEADS))
    kvspec = lambda off: ((MEM_LEN, dh), lambda i: (i // MEM_HEADS, off + i % MEM_HEADS))
    return _side(
        lambda blk, *refs: _xattn_main_block(*refs), BATCH * MEM_HEADS,
        args=[cq, kv, kv, scg],
        in_specs=[qspec(cq_blk), kvspec(0), kvspec(MEM_HEADS), qspec(scg_blk)],
        out_specs=[qspec(0)],
        out_shape=[jax.ShapeDtypeStruct((M_MAIN, MEM_WIDTH), BF16)])


XATTN_TAIL_ROWS = 4


def _xattn_tail_block(q_ref, g_ref, k_ref, v_ref, o_ref):
    for r in range(q_ref.shape[0]):
        q = q_ref[r] * (MEM_HEAD_DIM ** -0.5)
        sc = jnp.sum(k_ref[r] * q[None], axis=-1, keepdims=True)
        e = jnp.exp(sc - jnp.max(sc, axis=0, keepdims=True))
        o_ref[r] = jnp.sum(e * v_ref[r], axis=0) / jnp.sum(e, axis=0) * g_ref[r]


def _xattn_tail_side(cq, scg, ck, cv):
    nb = XATTN_TAIL_ROWS
    shape3 = (M_TAIL, MEM_HEADS, MEM_HEAD_DIM)
    row = ((nb, MEM_HEADS, MEM_HEAD_DIM), lambda i: (i, 0, 0))
    kv = ((None, nb, MEM_LEN, MEM_HEADS, MEM_HEAD_DIM), lambda i: (0, i, 0, 0, 0))
    return _side(
        lambda blk, *refs: _xattn_tail_block(*refs), DEC_BATCH // nb,
        args=[cq.reshape(shape3), scg.reshape(shape3), ck, cv],
        in_specs=[row, row, kv, kv], out_specs=[row],
        out_shape=[jax.ShapeDtypeStruct(shape3, F32)])


def kernel(x_prompt, x_sample, state_ret, cache_mem_k, cache_mem_v, mem_prompt, g_pre, w_in, g_anorm,
           w_s, b_s, g_ret, g_mem, w_mem_kv, w_out_a, w_out_b, w_out_c, w_out, g_final):
    assert DEPTH == 1 and w_in.shape == (DEPTH, D_MODEL, IN_WIDTH)
    xp = x_prompt.reshape(M_MAIN, D_MODEL)
    assert x_sample.shape == (M_TAIL, 1, D_MODEL)
    xs = x_sample
    win = w_in.reshape(D_MODEL, IN_WIDTH)

    h_m = _rmsnorm(xp, g_pre[0], 512, BF16)
    h_t = _rmsnorm(xs, g_pre[0], M_TAIL, BF16)
    cos_m, sin_m, cos_t, sin_t = _rope_tables()

    hm = _rmsnorm(mem_prompt.reshape(BATCH * MEM_LEN, D_MODEL), g_mem[0], 512, BF16)
    wkv = w_mem_kv.reshape(D_MODEL, 2 * MEM_WIDTH)
    (kv,) = _wres_matmul([(hm, None, wkv, 0)], [], [(F32, None)], n_cols=2 * MEM_WIDTH, tn=1024,
                         tm=512, sub=256, epilogue=_epi_identity, name="mem_kv")

    tn = tm = 1024

    def inproj(cols, n_cols, epilogue, out_dtypes, name, tn=tn, exts=(), **kw):
        return _wres_matmul([(h_m, h_t, win, c) for c in cols], list(exts), [out_dtypes],
                            n_cols=n_cols, tn=tn, tm=tm, sub=256, epilogue=epilogue, name=name, **kw)

    second = RET_WIDTH // tn
    id_m, id_t = inproj([((0, OFF_RV), (second, OFF_CQ))], RET_WIDTH + MEM_WIDTH, _epi_identity,
                        (BF16, F32), "inproj_v_cq")
    silu_m, silu_t = inproj([((0, OFF_RG), (second, OFF_CG))], RET_WIDTH + MEM_WIDTH, _epi_silu,
                            (BF16, F32), "inproj_rg_cg")
    gv_m, gv_t = inproj([OFF_AV], A_WIDTH, _epi_gelu, (BF16, F32), "inproj_gv")
    ug_m, ug_t = inproj([OFF_AU, OFF_AG], A_WIDTH, _epi_gelu_silu, (BF16, F32), "inproj_ug", tn=512)

    rope = [(cos_m, cos_t, "rope", 0), (sin_m, sin_t, "rope", 0)]
    q_m, q_t, a_m = inproj(
        [OFF_RQ], RET_WIDTH, functools.partial(_epi_rope, 1.0), (BF16, F32), "inproj_q", exts=rope,
        tn=512, side=_spatial_main_side(ug_m, (gv_m, 0), g_anorm[0], w_s[0], b_s[0]))
    a_t, vn_t = _spatial_tail(ug_t, gv_t, 0, g_anorm[0], w_s[0], b_s[0])
    mem_blk = RET_WIDTH // MEM_HEAD_DIM
    kt_m, k_t, c_m = inproj(
        [OFF_RK], RET_WIDTH, functools.partial(_epi_rope, RET_HEAD_DIM ** -0.5), (BF16, F32),
        "inproj_k", exts=rope, main_transposed=True,
        side=_xattn_main_side((id_m, mem_blk), (silu_m, mem_blk), kv),
        vmem_bytes=VMEM_HOST_BYTES)
    gs_m, gs_t, b_t, ret_s = inproj(
        [OFF_GATES], N_BRANCH * D_MODEL, _epi_sigmoid, (BF16, BF16), "inproj_gates",
        side=_ret_tail_side(q_t, k_t, (id_t, 0), (silu_t, 0), g_ret[0], state_ret),
        vmem_bytes=VMEM_HOST_BYTES)

    b_m, ret_p, c_t3 = _ret_main(
        q_m, kt_m, (id_m, 0), (silu_m, 0), g_ret[0],
        side=_xattn_tail_side(id_t[:, RET_WIDTH:], silu_t[:, RET_WIDTH:], cache_mem_k, cache_mem_v))
    c_t = c_t3.reshape(M_TAIL, MEM_WIDTH)

    merged_m, merged_t = _wres_matmul(
        [(a_m, a_t, w_out_a.reshape(A_WIDTH, D_MODEL), 0),
         (b_m, b_t, w_out_b.reshape(RET_WIDTH, D_MODEL), 0),
         (c_m, c_t, w_out_c.reshape(MEM_WIDTH, D_MODEL), 0)],
        [(gs_m, gs_t, "tile", b * D_MODEL) for b in range(N_BRANCH)],
        [(BF16, BF16)], n_cols=D_MODEL, tn=512, tm=1024, sub=256, epilogue=_epi_merge, name="merge")
    y_m, y_t = _wres_matmul(
        [(merged_m, merged_t, w_out.reshape(D_MODEL, D_MODEL), 0)],
        [(xp, xs, "tile", 0)], [(F32, F32)], n_cols=D_MODEL, tn=1024, tm=1024, sub=256,
        epilogue=_epi_residual, name="outproj", vmem_bytes=VMEM_HOST_BYTES)
    y_prompt = _rmsnorm(y_m, g_final, 512, F32).reshape(BATCH, SEQ, D_MODEL)
    assert DEC_SEQ == 1
    y_sample = _rmsnorm(y_t, g_final, M_TAIL, F32, out_3d=True)

    mem_shape = (DEPTH, BATCH, MEM_LEN, MEM_HEADS, MEM_HEAD_DIM)
    return (y_prompt, y_sample, ret_p,
            kv[:, :MEM_WIDTH].reshape(mem_shape), kv[:, MEM_WIDTH:].reshape(mem_shape),
            ret_s, vn_t)
```
